```python
import jax, jax.numpy as jnp
from jax import lax
import numpy as np

D_MODEL = 1024
BATCH = 4
SEQ = 4096
DEPTH = 4

HEAD_DIM = 64
ATTN_WIDTH = D_MODEL // 2
CONV_WIDTH = D_MODEL - ATTN_WIDTH
N_ATTN_HEADS = ATTN_WIDTH // HEAD_DIM
N_CONV_GROUPS = CONV_WIDTH // HEAD_DIM
IN_PROJ_WIDTH = 3 * ATTN_WIDTH + 3 * CONV_WIDTH
CONV_K = 3
ROPE_DIM = HEAD_DIM // 4
ROPE_THETA = 500000.0
DILATED_BRANCHES = ((128, 1), (512, 4), (2048, 16))
FFN_HIDDEN = ((8 * D_MODEL // 3 + 255) // 256) * 256
RMS_EPS = 1e-6
NEG_INF = -1e30

kernel_name = "hybrid_dilated_attn_shortconv_encoder"


def rms_norm(x, g):
    xf = x.astype(jnp.float32)
    y = xf * lax.rsqrt(jnp.mean(xf * xf, axis=-1, keepdims=True) + RMS_EPS)
    return (y * g.astype(jnp.float32)).astype(x.dtype)


def rotary_tables(positions):
    inv_freq = ROPE_THETA ** (-jnp.arange(0, ROPE_DIM, 2, dtype=jnp.float32) / ROPE_DIM)
    ang = positions.astype(jnp.float32)[..., None] * inv_freq
    return jnp.cos(ang)[:, None], jnp.sin(ang)[:, None]


def apply_partial_rotary(t, cos, sin):
    tf = t.astype(jnp.float32)
    half = ROPE_DIM // 2
    t1 = tf[..., :half]
    t2 = tf[..., half:ROPE_DIM]
    out = jnp.concatenate([t1 * cos - t2 * sin, t2 * cos + t1 * sin, tf[..., ROPE_DIM:]], axis=-1)
    return out.astype(t.dtype)


def banded_attention(q, k, v, half):
    L, dh = q.shape[-2], q.shape[-1]
    lead = q.shape[:-2]
    nb = -(-L // half)
    lp = nb * half
    pad_q = [(0, 0)] * len(lead) + [(0, lp - L), (0, 0)]
    pad_kv = [(0, 0)] * len(lead) + [(half, lp - L + half), (0, 0)]
    qb = jnp.pad(q, pad_q).reshape(*lead, nb, half, dh).astype(jnp.float32)

    def windows(t):
        tb = jnp.pad(t, pad_kv).reshape(*lead, nb + 2, half, dh).astype(jnp.float32)
        return jnp.concatenate([tb[..., :-2, :, :], tb[..., 1:-1, :, :], tb[..., 2:, :, :]], axis=-2)

    kw = windows(k)
    vw = windows(v)
    s = jnp.einsum('...nqd,...nkd->...nqk', qb, kw) * (dh ** -0.5)
    qi = jnp.arange(nb)[:, None] * half + jnp.arange(half)[None, :]
    ki = jnp.arange(nb)[:, None] * half + jnp.arange(3 * half)[None, :] - half
    valid = (jnp.abs(qi[:, :, None] - ki[:, None, :]) <= half) & ((ki >= 0) & (ki < L))[:, None, :]
    s = jnp.where(valid, s, NEG_INF)
    lse = jax.nn.logsumexp(s, axis=-1)
    p = jnp.exp(s - lse[..., None])
    o = jnp.einsum('...nqk,...nkd->...nqd', p, vw)
    o = o.reshape(*lead, lp, dh)[..., :L, :]
    lse = lse.reshape(*lead, lp)[..., :L]
    return o, lse


def dilated_mixture_attention(q, k, v):
    b, h, s, dh = q.shape
    outs, lses = [], []
    for window, dil in DILATED_BRANCHES:
        L = s // dil
        half = window // (2 * dil)

        def by_residue(t):
            return t.reshape(b, h, L, dil, dh).swapaxes(2, 3)

        o, lse = banded_attention(by_residue(q), by_residue(k), by_residue(v), half)
        outs.append(o.swapaxes(2, 3).reshape(b, h, s, dh))
        lses.append(lse.swapaxes(2, 3).reshape(b, h, s))
    w = jax.nn.softmax(jnp.stack(lses, axis=0), axis=0)
    o = jnp.sum(w[..., None] * jnp.stack(outs, axis=0), axis=0)
    return o.astype(q.dtype)


def short_conv(u, w):
    c = u.shape[-1]
    return lax.conv_general_dilated(
        u, w.reshape(CONV_K, 1, c).astype(u.dtype), window_strides=(1,),
        padding=[(CONV_K // 2, CONV_K // 2)], dimension_numbers=('NWC', 'WIO', 'NWC'),
        feature_group_count=c)


def setup_inputs(seed: int = 0) -> dict:
    key = jax.random.key(seed)
    ks = jax.random.split(key, 16)
    f32 = jnp.float32

    def gain(k, shape):
        return 1.0 + 0.02 * jax.random.normal(k, shape, f32)

    x = jax.random.normal(ks[0], (BATCH, SEQ, D_MODEL), f32)
    offsets = jax.random.randint(ks[1], (BATCH, 1), 0, 1024, dtype=jnp.int32)
    positions = (jnp.arange(SEQ, dtype=jnp.int32)[None, :] + offsets).astype(jnp.int32)
    return {
        "x": x,
        "positions": positions,
        "pre_mix_norm": gain(ks[2], (DEPTH, D_MODEL)),
        "w_in": jax.random.normal(ks[3], (DEPTH, D_MODEL, IN_PROJ_WIDTH), f32) * D_MODEL ** -0.5,
        "conv_w": jax.random.normal(ks[4], (DEPTH, CONV_K, CONV_WIDTH), f32) * CONV_K ** -0.5,
        "attn_out_norm": gain(ks[5], (DEPTH, ATTN_WIDTH)),
        "conv_out_norm": gain(ks[6], (DEPTH, CONV_WIDTH)),
        "w_out": jax.random.normal(ks[7], (DEPTH, D_MODEL, D_MODEL), f32) * D_MODEL ** -0.5,
        "post_mix_norm": gain(ks[8], (DEPTH, D_MODEL)),
        "pre_ffn_norm": gain(ks[9], (DEPTH, D_MODEL)),
        "w_gate_up": jax.random.normal(ks[10], (DEPTH, D_MODEL, 2 * FFN_HIDDEN), f32) * D_MODEL ** -0.5,
        "w_down": jax.random.normal(ks[11], (DEPTH, FFN_HIDDEN, D_MODEL), f32) * FFN_HIDDEN ** -0.5,
        "post_ffn_norm": gain(ks[12], (DEPTH, D_MODEL)),
    }


def reference(x, positions, pre_mix_norm, w_in, conv_w, attn_out_norm, conv_out_norm,
              w_out, post_mix_norm, pre_ffn_norm, w_gate_up, w_down, post_ffn_norm):
    b, s, _ = x.shape
    cos, sin = rotary_tables(positions)
    split_points = np.cumsum([ATTN_WIDTH] * 3 + [CONV_WIDTH] * 2).tolist()

    def heads(t):
        return t.reshape(b, s, N_ATTN_HEADS, HEAD_DIM).transpose(0, 2, 1, 3)

    for l in range(DEPTH):
        h = rms_norm(x, pre_mix_norm[l])
        proj = jnp.einsum('bsd,de->bse', h, w_in[l])
        q, k, v, conv_u, gate_b, gate_c = jnp.split(proj, split_points, axis=-1)
        q = apply_partial_rotary(heads(q), cos, sin)
        k = apply_partial_rotary(heads(k), cos, sin)
        attn = dilated_mixture_attention(q, k, heads(v))
        attn = attn.transpose(0, 2, 1, 3).reshape(b, s, ATTN_WIDTH)
        conv_y = gate_b * short_conv(gate_c * conv_u, conv_w[l])
        merged = jnp.concatenate([rms_norm(attn, attn_out_norm[l]),
                                  rms_norm(conv_y, conv_out_norm[l])], axis=-1)
        mix = jnp.einsum('bse,ed->bsd', merged, w_out[l])
        x = x + rms_norm(mix, post_mix_norm[l])
        h = rms_norm(x, pre_ffn_norm[l])
        g, u = jnp.split(jnp.einsum('bsd,df->bsf', h, w_gate_up[l]), 2, axis=-1)
        f = jnp.einsum('bsf,fd->bsd', jax.nn.silu(g) * u, w_down[l])
        x = x + rms_norm(f, post_ffn_norm[l])
    return x
```

```python
import functools

import jax
import jax.numpy as jnp
from jax import lax
from jax.experimental import pallas as pl
from jax.experimental.pallas import tpu as pltpu

D_MODEL = 1024
HEAD_DIM = 64
ATTN_WIDTH = 512
CONV_WIDTH = 512
N_ATTN_HEADS = 8
IN_PROJ_WIDTH = 3 * ATTN_WIDTH + 3 * CONV_WIDTH
CONV_K = 3
ROPE_DIM = 16
ROPE_THETA = 500000.0
DILATIONS = (1, 4, 16)
HALF = 64
FFN_HIDDEN = 2816
RMS_EPS = 1e-6
NEG_INF = -1e30

LANES = 128
ROW_TILE = 512
TQ = 128
NK = TQ + 2 * HALF
FFN_CHUNKS = ((0, 1536), (1536, 1280))

F32 = jnp.float32
BF16 = jnp.bfloat16


def _rms(x, g):
    return x * lax.rsqrt(jnp.mean(x * x, axis=-1, keepdims=True) + RMS_EPS) * g


def _in_proj_kernel(x_ref, g_ref, w_ref, c_ref, s1_ref, s2_ref,
                    q_ref, k_ref, v_ref, u_ref, gb_ref):
    h = _rms(x_ref[...], g_ref[...]).astype(BF16)

    def proj(c):
        return jnp.dot(h, w_ref[:, c * 512:(c + 1) * 512], preferred_element_type=F32)

    cos, sin_up, sin_dn = c_ref[...], s1_ref[...], s2_ref[...]

    def rotary(t):
        parts = []
        for j in range(ATTN_WIDTH // LANES):
            tj = t[:, j * LANES:(j + 1) * LANES]
            parts.append(tj * cos + pltpu.roll(tj, 8, 1) * sin_up
                         + pltpu.roll(tj, LANES - 8, 1) * sin_dn)
        return jnp.concatenate(parts, axis=1)

    q_ref[...] = (rotary(proj(0)) * (HEAD_DIM ** -0.5)).astype(BF16)
    k_ref[...] = rotary(proj(1)).astype(BF16)
    v_ref[...] = proj(2).astype(BF16)
    u_ref[...] = (proj(5) * proj(3)).astype(BF16)
    gb_ref[...] = proj(4).astype(BF16)


def _in_proj(x2, g, w, cos, sin_up, sin_dn, layer):
    n = x2.shape[0]
    row = lambda i: (i, 0)
    out = jax.ShapeDtypeStruct((n, 512), BF16)
    return pl.pallas_call(
        _in_proj_kernel,
        grid=(n // ROW_TILE,),
        in_specs=[
            pl.BlockSpec((ROW_TILE, D_MODEL), row),
            pl.BlockSpec((None, 1, D_MODEL), lambda i: (layer, 0, 0)),
            pl.BlockSpec((None, D_MODEL, IN_PROJ_WIDTH), lambda i: (layer, 0, 0),
                         pipeline_mode=pl.Buffered(1)),
            pl.BlockSpec((ROW_TILE, LANES), row),
            pl.BlockSpec((ROW_TILE, LANES), row),
            pl.BlockSpec((ROW_TILE, LANES), row),
        ],
        out_specs=[pl.BlockSpec((ROW_TILE, 512), row)] * 5,
        out_shape=[out] * 5,
        compiler_params=pltpu.CompilerParams(
            dimension_semantics=("parallel",), vmem_limit_bytes=40 * 2 ** 20),
        name="in_proj",
    )(x2, g, w, cos, sin_up, sin_dn)


def _attn_kernel(q_ref, k_ref, v_ref, o_ref, qf, kf, vf, ob, lb, bias, *, seq):
    pad = HALF * DILATIONS[-1]
    lane_q = lax.broadcasted_iota(jnp.int32, (TQ, LANES), 1)
    head_a = lane_q < HEAD_DIM
    head_a_k = lax.broadcasted_iota(jnp.int32, (NK, LANES), 1) < HEAD_DIM

    qf[...] = q_ref[0].astype(F32)
    zeros = jnp.zeros((pad, LANES), F32)
    for src, dst in ((k_ref, kf), (v_ref, vf)):
        dst[0:pad, :] = zeros
        dst[pad + seq:pad + seq + pad, :] = zeros
        dst[pad:pad + seq, :] = src[0].astype(F32)

    r = lax.broadcasted_iota(jnp.int32, (TQ, NK), 0)
    c = lax.broadcasted_iota(jnp.int32, (TQ, NK), 1)
    band = (c >= r) & (c <= r + 2 * HALF)
    bias[0] = jnp.where(band & (c >= HALF), 0.0, NEG_INF)
    bias[1] = jnp.where(band, 0.0, NEG_INF)
    bias[2] = jnp.where(band & (c < HALF + TQ), 0.0, NEG_INF)

    def rows(start, size, d):
        return pl.ds(start, size) if d == 1 else pl.ds(start, size, stride=d)

    for br, d in enumerate(DILATIONS):
        nblk = seq // d // TQ

        def block(i, res, br=br, d=d, nblk=nblk):
            q0 = res + d * TQ * i
            k0 = pad + res + d * (TQ * i - HALF)
            qb = qf[rows(q0, TQ, d), :]
            kb = kf[rows(k0, NK, d), :].astype(BF16)
            vb = vf[rows(k0, NK, d), :]
            b = bias[jnp.where(i == 0, 0, jnp.where(i == nblk - 1, 2, 1))]

            def one_head(is_a):
                qh = jnp.where(head_a == is_a, qb, 0.0).astype(BF16)
                s = lax.dot_general(qh, kb, (((1,), (1,)), ((), ())),
                                    preferred_element_type=F32) + b
                m = jnp.max(s, axis=1, keepdims=True)
                p = jnp.exp(s - m).astype(BF16)
                vh = jnp.where(head_a_k == is_a, vb, 1.0).astype(BF16)
                return m, jnp.dot(p, vh, preferred_element_type=F32)

            m_a, r_a = one_head(True)
            m_b, r_b = one_head(False)
            acc = jnp.where(head_a, r_a, r_b)
            l = pltpu.roll(jnp.where(head_a, r_b, r_a), HEAD_DIM, 1)
            m = jnp.where(head_a, m_a, m_b)
            ob[br, rows(q0, TQ, d), :] = acc / l
            lb[br, rows(q0, TQ, d), :] = m + jnp.log(l)

        def residue(res, carry, block=block, nblk=nblk):
            def body(i, c2):
                block(i, res)
                return c2
            return lax.fori_loop(0, nblk, body, carry)

        lax.fori_loop(0, d, residue, 0)

    chunk = 256

    def combine(j, carry):
        sl = pl.ds(pl.multiple_of(j * chunk, chunk), chunk)
        l0, l1, l2 = lb[0, sl, :], lb[1, sl, :], lb[2, sl, :]
        mx = jnp.maximum(jnp.maximum(l0, l1), l2)
        w0, w1, w2 = jnp.exp(l0 - mx), jnp.exp(l1 - mx), jnp.exp(l2 - mx)
        num = w0 * ob[0, sl, :] + w1 * ob[1, sl, :] + w2 * ob[2, sl, :]
        o_ref[0, sl, :] = (num / (w0 + w1 + w2)).astype(o_ref.dtype)
        return carry

    lax.fori_loop(0, seq // chunk, combine, 0)


def _attention(q, k, v):
    b, s, _ = q.shape
    pad = HALF * DILATIONS[-1]
    spec = pl.BlockSpec((1, s, LANES), lambda i, j: (i, 0, j))
    return pl.pallas_call(
        functools.partial(_attn_kernel, seq=s),
        grid=(b, ATTN_WIDTH // LANES),
        in_specs=[spec, spec, spec],
        out_specs=spec,
        out_shape=jax.ShapeDtypeStruct((b, s, ATTN_WIDTH), BF16),
        scratch_shapes=[
            pltpu.VMEM((s, LANES), F32),
            pltpu.VMEM((s + 2 * pad, LANES), F32),
            pltpu.VMEM((s + 2 * pad, LANES), F32),
            pltpu.VMEM((len(DILATIONS), s, LANES), F32),
            pltpu.VMEM((len(DILATIONS), s, LANES), F32),
            pltpu.VMEM((3, TQ, NK), F32),
        ],
        compiler_params=pltpu.CompilerParams(
            dimension_semantics=("parallel", "parallel"), vmem_limit_bytes=48 * 2 ** 20),
        name="attn",
    )(q, k, v)


def _mix_kernel(a_ref, u_ref, up_ref, un_ref, gb_ref, x_ref, cw_ref, ga_ref, gc_ref,
                w_ref, gp_ref, o_ref, *, tiles_per_seq):
    t = pl.program_id(0) % tiles_per_seq
    u = u_ref[...].astype(F32)
    tm = u.shape[0]
    prev_row = jnp.where(t == 0, 0.0, up_ref[7:8, :].astype(F32))
    next_row = jnp.where(t == tiles_per_seq - 1, 0.0, un_ref[0:1, :].astype(F32))
    row = lax.broadcasted_iota(jnp.int32, u.shape, 0)
    u_prev = jnp.where(row == 0, prev_row, pltpu.roll(u, 1, 0))
    u_next = jnp.where(row == tm - 1, next_row, pltpu.roll(u, tm - 1, 0))
    y = cw_ref[0:1, :] * u_prev + cw_ref[1:2, :] * u + cw_ref[2:3, :] * u_next
    conv = gb_ref[...].astype(F32) * y
    na = _rms(a_ref[...].astype(F32), ga_ref[...]).astype(BF16)
    nc = _rms(conv, gc_ref[...]).astype(BF16)
    mix = (jnp.dot(na, w_ref[0:ATTN_WIDTH, :], preferred_element_type=F32)
           + jnp.dot(nc, w_ref[ATTN_WIDTH:, :], preferred_element_type=F32))
    o_ref[...] = x_ref[...] + _rms(mix, gp_ref[...])


def _mix(attn2, u, gb, x2, conv_w, g_attn, g_conv, w_out, g_post, layer, seq):
    n = x2.shape[0]
    row = lambda i: (i, 0)
    halo = ROW_TILE // 8
    vec = lambda width: pl.BlockSpec((None, 1, width), lambda i: (layer, 0, 0))
    return pl.pallas_call(
        functools.partial(_mix_kernel, tiles_per_seq=seq // ROW_TILE),
        grid=(n // ROW_TILE,),
        in_specs=[
            pl.BlockSpec((ROW_TILE, ATTN_WIDTH), row),
            pl.BlockSpec((ROW_TILE, CONV_WIDTH), row),
            pl.BlockSpec((8, CONV_WIDTH), lambda i: (jnp.maximum(i * halo - 1, 0), 0)),
            pl.BlockSpec((8, CONV_WIDTH), lambda i: (jnp.minimum((i + 1) * halo, n // 8 - 1), 0)),
            pl.BlockSpec((ROW_TILE, CONV_WIDTH), row),
            pl.BlockSpec((ROW_TILE, D_MODEL), row),
            pl.BlockSpec((None, CONV_K, CONV_WIDTH), lambda i: (layer, 0, 0)),
            vec(ATTN_WIDTH),
            vec(CONV_WIDTH),
            pl.BlockSpec((None, D_MODEL, D_MODEL), lambda i: (layer, 0, 0),
                         pipeline_mode=pl.Buffered(1)),
            vec(D_MODEL),
        ],
        out_specs=pl.BlockSpec((ROW_TILE, D_MODEL), row),
        out_shape=jax.ShapeDtypeStruct((n, D_MODEL), F32),
        compiler_params=pltpu.CompilerParams(
            dimension_semantics=("parallel",), vmem_limit_bytes=40 * 2 ** 20),
        name="mix",
    )(attn2, u, u, u, gb, x2, conv_w, g_attn, g_conv, w_out, g_post)


def _ffn_kernel(x_ref, g1_ref, wgu_ref, wd_ref, g2_ref, o_ref):
    x = x_ref[...]
    h = _rms(x, g1_ref[...]).astype(BF16)
    f = None
    for start, size in FFN_CHUNKS:
        g = jnp.dot(h, wgu_ref[:, start:start + size], preferred_element_type=F32)
        u = jnp.dot(h, wgu_ref[:, FFN_HIDDEN + start:FFN_HIDDEN + start + size],
                    preferred_element_type=F32)
        act = (g * jax.nn.sigmoid(g) * u).astype(BF16)
        part = jnp.dot(act, wd_ref[start:start + size, :], preferred_element_type=F32)
        f = part if f is None else f + part
    o_ref[...] = x + _rms(f, g2_ref[...])


def _ffn(x2, g_pre, w_gu, w_down, g_post, layer):
    n = x2.shape[0]
    row = lambda i: (i, 0)
    vec = pl.BlockSpec((None, 1, D_MODEL), lambda i: (layer, 0, 0))
    return pl.pallas_call(
        _ffn_kernel,
        grid=(n // ROW_TILE,),
        in_specs=[
            pl.BlockSpec((ROW_TILE, D_MODEL), row),
            vec,
            pl.BlockSpec((None, D_MODEL, 2 * FFN_HIDDEN), lambda i: (layer, 0, 0),
                         pipeline_mode=pl.Buffered(1)),
            pl.BlockSpec((None, FFN_HIDDEN, D_MODEL), lambda i: (layer, 0, 0),
                         pipeline_mode=pl.Buffered(1)),
            vec,
        ],
        out_specs=pl.BlockSpec((ROW_TILE, D_MODEL), row),
        out_shape=jax.ShapeDtypeStruct((n, D_MODEL), F32),
        compiler_params=pltpu.CompilerParams(
            dimension_semantics=("parallel",), vmem_limit_bytes=56 * 2 ** 20),
        name="ffn",
    )(x2, g_pre, w_gu, w_down, g_post)


def _rotary_tables(positions):
    half = ROPE_DIM // 2
    inv_freq = ROPE_THETA ** (-jnp.arange(0, ROPE_DIM, 2, dtype=F32) / ROPE_DIM)
    ang = positions.astype(F32).reshape(-1, 1) * inv_freq
    cos, sin = jnp.cos(ang), jnp.sin(ang)
    n = ang.shape[0]
    rest = HEAD_DIM - ROPE_DIM
    one = jnp.ones((n, rest), F32)
    zero = jnp.zeros((n, rest), F32)
    z8 = jnp.zeros((n, half), F32)
    c = jnp.concatenate([cos, cos, one], axis=1)
    s_up = jnp.concatenate([z8, sin, zero], axis=1)
    s_dn = jnp.concatenate([-sin, z8, zero], axis=1)
    tile2 = lambda t: jnp.concatenate([t, t], axis=1)
    return tile2(c), tile2(s_up), tile2(s_dn)


def kernel(x, positions, pre_mix_norm, w_in, conv_w, attn_out_norm, conv_out_norm, w_out,
           post_mix_norm, pre_ffn_norm, w_gate_up, w_down, post_ffn_norm):
    b, s, d = x.shape
    depth = w_in.shape[0]
    n = b * s
    cos, sin_up, sin_dn = _rotary_tables(positions)
    vec = lambda g: g.reshape(depth, 1, g.shape[-1])
    w_in_b, w_out_b = w_in.astype(BF16), w_out.astype(BF16)
    w_gu_b, w_down_b = w_gate_up.astype(BF16), w_down.astype(BF16)
    x2 = x.reshape(n, d)
    for l in range(depth):
        q, k, v, u, gb = _in_proj(x2, vec(pre_mix_norm), w_in_b, cos, sin_up, sin_dn, l)
        attn = _attention(q.reshape(b, s, ATTN_WIDTH), k.reshape(b, s, ATTN_WIDTH),
                          v.reshape(b, s, ATTN_WIDTH))
        x2 = _mix(attn.reshape(n, ATTN_WIDTH), u, gb, x2, conv_w, vec(attn_out_norm),
                  vec(conv_out_norm), w_out_b, vec(post_mix_norm), l, s)
        x2 = _ffn(x2, vec(pre_ffn_norm), w_gu_b, w_down_b, vec(post_ffn_norm), l)
    return x2.reshape(b, s, d)
```

```python
import functools

import jax
import jax.numpy as jnp
from jax import lax
from jax.experimental import pallas as pl
from jax.experimental.pallas import tpu as pltpu

D_MODEL = 1024
HEAD_DIM = 64
ATTN_WIDTH = 512
CONV_WIDTH = 512
N_ATTN_HEADS = 8
IN_PROJ_WIDTH = 3 * ATTN_WIDTH + 3 * CONV_WIDTH
CONV_K = 3
ROPE_DIM = 16
ROPE_THETA = 500000.0
DILATIONS = (1, 4, 16)
HALF = 64
FFN_HIDDEN = 2816
RMS_EPS = 1e-6
NEG_INF = -1e30
LOG2_E = 1.4426950408889634

LANES = 128
ROW_TILE = 512
TQ = 128
NK = TQ + 2 * HALF
GROUP = 8
FFN_CHUNKS = ((0, 1536), (1536, 1280))

F32 = jnp.float32
BF16 = jnp.bfloat16


def _rms(x, g):
    return x * lax.rsqrt(jnp.mean(x * x, axis=-1, keepdims=True) + RMS_EPS) * g


def _in_proj_kernel(x_ref, g_ref, w_ref, c_ref, s1_ref, s2_ref,
                    q_ref, k_ref, v_ref, u_ref, gb_ref):
    h = _rms(x_ref[...], g_ref[...]).astype(BF16)

    def proj(c):
        return jnp.dot(h, w_ref[:, c * 512:(c + 1) * 512], preferred_element_type=F32)

    cos, sin_up, sin_dn = c_ref[...], s1_ref[...], s2_ref[...]

    def rotary(t):
        parts = []
        for j in range(ATTN_WIDTH // LANES):
            tj = t[:, j * LANES:(j + 1) * LANES]
            parts.append(tj * cos + pltpu.roll(tj, 8, 1) * sin_up
                         + pltpu.roll(tj, LANES - 8, 1) * sin_dn)
        return jnp.concatenate(parts, axis=1)

    q_ref[...] = (rotary(proj(0)) * (HEAD_DIM ** -0.5 * LOG2_E)).astype(BF16)
    k_ref[...] = rotary(proj(1)).astype(BF16)
    v_ref[...] = proj(2).astype(BF16)
    u_ref[...] = (proj(5) * proj(3)).astype(BF16)
    gb_ref[...] = proj(4).astype(BF16)


def _in_proj(x2, g, w, cos, sin_up, sin_dn, layer):
    n = x2.shape[0]
    row = lambda i: (i, 0)
    out = jax.ShapeDtypeStruct((n, 512), BF16)
    return pl.pallas_call(
        _in_proj_kernel,
        grid=(n // ROW_TILE,),
        in_specs=[
            pl.BlockSpec((ROW_TILE, D_MODEL), row),
            pl.BlockSpec((None, 1, D_MODEL), lambda i: (layer, 0, 0)),
            pl.BlockSpec((None, D_MODEL, IN_PROJ_WIDTH), lambda i: (layer, 0, 0),
                         pipeline_mode=pl.Buffered(1)),
            pl.BlockSpec((ROW_TILE, LANES), row),
            pl.BlockSpec((ROW_TILE, LANES), row),
            pl.BlockSpec((ROW_TILE, LANES), row),
        ],
        out_specs=[pl.BlockSpec((ROW_TILE, 512), row)] * 5,
        out_shape=[out] * 5,
        compiler_params=pltpu.CompilerParams(
            dimension_semantics=("parallel",), vmem_limit_bytes=40 * 2 ** 20),
        name="in_proj",
    )(x2, g, w, cos, sin_up, sin_dn)


def _attn_kernel(q_ref, k_ref, v_ref, o_ref, qf, kf, vf, ob, lb, bias, *, seq):
    pad = HALF * DILATIONS[-1]
    lane_q = lax.broadcasted_iota(jnp.int32, (TQ, LANES), 1)
    head_a = lane_q < HEAD_DIM
    head_a_k = lax.broadcasted_iota(jnp.int32, (NK, LANES), 1) < HEAD_DIM

    qf[...] = q_ref[0].astype(F32)
    zeros = jnp.zeros((pad, LANES), F32)
    for src, dst in ((k_ref, kf), (v_ref, vf)):
        dst[0:pad, :] = zeros
        dst[pad + seq:pad + seq + pad, :] = zeros
        dst[pad:pad + seq, :] = src[0].astype(F32)

    r = lax.broadcasted_iota(jnp.int32, (TQ, NK), 0)
    c = lax.broadcasted_iota(jnp.int32, (TQ, NK), 1)
    band = (c >= r) & (c <= r + 2 * HALF)
    bias[0] = jnp.where(band & (c >= HALF), 0.0, NEG_INF)
    bias[1] = jnp.where(band, 0.0, NEG_INF)
    bias[2] = jnp.where(band & (c < HALF + TQ), 0.0, NEG_INF)

    def rows(start, size, d):
        return pl.ds(start, size) if d == 1 else pl.ds(start, size, stride=d)

    for br, d in enumerate(DILATIONS):
        nblk = seq // d // TQ

        def block(i, res, bias_idx, br=br, d=d):
            q0 = res + d * TQ * i
            k0 = pad + res + d * (TQ * i - HALF)
            qb = qf[rows(q0, TQ, d), :]
            kb = kf[rows(k0, NK, d), :].astype(BF16)
            vb = vf[rows(k0, NK, d), :].astype(BF16)
            b = bias[bias_idx]

            def one_head(sel_q, sel_k):
                qh = jnp.where(sel_q, qb, 0.0).astype(BF16)
                s = lax.dot_general(qh, kb, (((1,), (1,)), ((), ())),
                                    preferred_element_type=F32) + b
                m = jnp.max(s, axis=1, keepdims=True)
                p = jnp.exp2(s - m).astype(BF16)
                vh = jnp.where(sel_k, vb, jnp.ones_like(vb))
                return m, jnp.dot(p, vh, preferred_element_type=F32)

            m_a, r_a = one_head(head_a, head_a_k)
            m_b, r_b = one_head(~head_a, ~head_a_k)
            acc = jnp.where(head_a, r_a, r_b)
            l = pltpu.roll(jnp.where(head_a, r_b, r_a), HEAD_DIM, 1)
            m = jnp.where(head_a, m_a, m_b)
            ob[br, rows(q0, TQ, d), :] = acc / l
            lb[br, rows(q0, TQ, d), :] = m + jnp.log2(l)

        if nblk >= GROUP:
            ngrp = nblk // GROUP

            def group(t, carry, block=block, ngrp=ngrp):
                res, g = t // ngrp, t % ngrp
                for j in range(GROUP):
                    if j == 0:
                        idx = 0 if ngrp == 1 else jnp.where(g == 0, 0, 1)
                    elif j == GROUP - 1:
                        idx = 2 if ngrp == 1 else jnp.where(g == ngrp - 1, 2, 1)
                    else:
                        idx = 1
                    block(g * GROUP + j, res, idx)
                return carry

            lax.fori_loop(0, d * ngrp, group, 0)
        else:
            assert nblk == 2
            per = GROUP // nblk

            def group(t, carry, block=block, per=per):
                for jr in range(per):
                    block(0, t * per + jr, 0)
                    block(1, t * per + jr, 2)
                return carry

            lax.fori_loop(0, d // per, group, 0)

    chunk = 256

    def combine(j, carry):
        sl = pl.ds(pl.multiple_of(j * chunk, chunk), chunk)
        l0, l1, l2 = lb[0, sl, :], lb[1, sl, :], lb[2, sl, :]
        mx = jnp.maximum(jnp.maximum(l0, l1), l2)
        w0, w1, w2 = jnp.exp2(l0 - mx), jnp.exp2(l1 - mx), jnp.exp2(l2 - mx)
        num = w0 * ob[0, sl, :] + w1 * ob[1, sl, :] + w2 * ob[2, sl, :]
        o_ref[0, sl, :] = (num / (w0 + w1 + w2)).astype(o_ref.dtype)
        return carry

    lax.fori_loop(0, seq // chunk, combine, 0)


def _attention(q, k, v):
    b, s, _ = q.shape
    pad = HALF * DILATIONS[-1]
    spec = pl.BlockSpec((1, s, LANES), lambda i, j: (i, 0, j))
    return pl.pallas_call(
        functools.partial(_attn_kernel, seq=s),
        grid=(b, ATTN_WIDTH // LANES),
        in_specs=[spec, spec, spec],
        out_specs=spec,
        out_shape=jax.ShapeDtypeStruct((b, s, ATTN_WIDTH), BF16),
        scratch_shapes=[
            pltpu.VMEM((s, LANES), F32),
            pltpu.VMEM((s + 2 * pad, LANES), F32),
            pltpu.VMEM((s + 2 * pad, LANES), F32),
            pltpu.VMEM((len(DILATIONS), s, LANES), F32),
            pltpu.VMEM((len(DILATIONS), s, LANES), F32),
            pltpu.VMEM((3, TQ, NK), F32),
        ],
        compiler_params=pltpu.CompilerParams(
            dimension_semantics=("parallel", "parallel"), vmem_limit_bytes=48 * 2 ** 20),
        name="attn",
    )(q, k, v)


def _mix_kernel(a_ref, u_ref, up_ref, un_ref, gb_ref, x_ref, cw_ref, ga_ref, gc_ref,
                w_ref, gp_ref, o_ref, *, tiles_per_seq):
    t = pl.program_id(0) % tiles_per_seq
    u = u_ref[...].astype(F32)
    tm = u.shape[0]
    prev_row = jnp.where(t == 0, 0.0, up_ref[7:8, :].astype(F32))
    next_row = jnp.where(t == tiles_per_seq - 1, 0.0, un_ref[0:1, :].astype(F32))
    row = lax.broadcasted_iota(jnp.int32, u.shape, 0)
    u_prev = jnp.where(row == 0, prev_row, pltpu.roll(u, 1, 0))
    u_next = jnp.where(row == tm - 1, next_row, pltpu.roll(u, tm - 1, 0))
    y = cw_ref[0:1, :] * u_prev + cw_ref[1:2, :] * u + cw_ref[2:3, :] * u_next
    conv = gb_ref[...].astype(F32) * y
    na = _rms(a_ref[...].astype(F32), ga_ref[...]).astype(BF16)
    nc = _rms(conv, gc_ref[...]).astype(BF16)
    mix = (jnp.dot(na, w_ref[0:ATTN_WIDTH, :], preferred_element_type=F32)
           + jnp.dot(nc, w_ref[ATTN_WIDTH:, :], preferred_element_type=F32))
    o_ref[...] = x_ref[...] + _rms(mix, gp_ref[...])


def _mix(attn2, u, gb, x2, conv_w, g_attn, g_conv, w_out, g_post, layer, seq):
    n = x2.shape[0]
    row = lambda i: (i, 0)
    halo = ROW_TILE // 8
    vec = lambda width: pl.BlockSpec((None, 1, width), lambda i: (layer, 0, 0))
    return pl.pallas_call(
        functools.partial(_mix_kernel, tiles_per_seq=seq // ROW_TILE),
        grid=(n // ROW_TILE,),
        in_specs=[
            pl.BlockSpec((ROW_TILE, ATTN_WIDTH), row),
            pl.BlockSpec((ROW_TILE, CONV_WIDTH), row),
            pl.BlockSpec((8, CONV_WIDTH), lambda i: (jnp.maximum(i * halo - 1, 0), 0)),
            pl.BlockSpec((8, CONV_WIDTH), lambda i: (jnp.minimum((i + 1) * halo, n // 8 - 1), 0)),
            pl.BlockSpec((ROW_TILE, CONV_WIDTH), row),
            pl.BlockSpec((ROW_TILE, D_MODEL), row),
            pl.BlockSpec((None, CONV_K, CONV_WIDTH), lambda i: (layer, 0, 0)),
            vec(ATTN_WIDTH),
            vec(CONV_WIDTH),
            pl.BlockSpec((None, D_MODEL, D_MODEL), lambda i: (layer, 0, 0),
                         pipeline_mode=pl.Buffered(1)),
            vec(D_MODEL),
        ],
        out_specs=pl.BlockSpec((ROW_TILE, D_MODEL), row),
        out_shape=jax.ShapeDtypeStruct((n, D_MODEL), F32),
        compiler_params=pltpu.CompilerParams(
            dimension_semantics=("parallel",), vmem_limit_bytes=40 * 2 ** 20),
        name="mix",
    )(attn2, u, u, u, gb, x2, conv_w, g_attn, g_conv, w_out, g_post)


def _ffn_kernel(x_ref, g1_ref, wgu_ref, wd_ref, g2_ref, o_ref):
    x = x_ref[...]
    h = _rms(x, g1_ref[...]).astype(BF16)
    f = None
    for start, size in FFN_CHUNKS:
        g = jnp.dot(h, wgu_ref[:, start:start + size], preferred_element_type=F32)
        u = jnp.dot(h, wgu_ref[:, FFN_HIDDEN + start:FFN_HIDDEN + start + size],
                    preferred_element_type=F32)
        act = (g * jax.nn.sigmoid(g) * u).astype(BF16)
        part = jnp.dot(act, wd_ref[start:start + size, :], preferred_element_type=F32)
        f = part if f is None else f + part
    o_ref[...] = x + _rms(f, g2_ref[...])


def _ffn(x2, g_pre, w_gu, w_down, g_post, layer):
    n = x2.shape[0]
    row = lambda i: (i, 0)
    vec = pl.BlockSpec((None, 1, D_MODEL), lambda i: (layer, 0, 0))
    return pl.pallas_call(
        _ffn_kernel,
        grid=(n // ROW_TILE,),
        in_specs=[
            pl.BlockSpec((ROW_TILE, D_MODEL), row),
            vec,
            pl.BlockSpec((None, D_MODEL, 2 * FFN_HIDDEN), lambda i: (layer, 0, 0),
                         pipeline_mode=pl.Buffered(1)),
            pl.BlockSpec((None, FFN_HIDDEN, D_MODEL), lambda i: (layer, 0, 0),
                         pipeline_mode=pl.Buffered(1)),
            vec,
        ],
        out_specs=pl.BlockSpec((ROW_TILE, D_MODEL), row),
        out_shape=jax.ShapeDtypeStruct((n, D_MODEL), F32),
        compiler_params=pltpu.CompilerParams(
            dimension_semantics=("parallel",), vmem_limit_bytes=56 * 2 ** 20),
        name="ffn",
    )(x2, g_pre, w_gu, w_down, g_post)


def _rotary_tables(positions):
    half = ROPE_DIM // 2
    inv_freq = ROPE_THETA ** (-jnp.arange(0, ROPE_DIM, 2, dtype=F32) / ROPE_DIM)
    in_head = jnp.arange(LANES) % HEAD_DIM
    freq = jnp.where(in_head < ROPE_DIM, inv_freq[in_head % half], 0.0)
    ang = positions.astype(F32).reshape(-1, 1) * freq
    cos, sin = jnp.cos(ang), jnp.sin(ang)
    s_up = jnp.where(in_head >= half, sin, 0.0)
    s_dn = jnp.where(in_head < half, -sin, 0.0)
    return cos, s_up, s_dn


def kernel(x, positions, pre_mix_norm, w_in, conv_w, attn_out_norm, conv_out_norm, w_out,
           post_mix_norm, pre_ffn_norm, w_gate_up, w_down, post_ffn_norm):
    b, s, d = x.shape
    depth = w_in.shape[0]
    n = b * s
    cos, sin_up, sin_dn = _rotary_tables(positions)
    vec = lambda g: g.reshape(depth, 1, g.shape[-1])
    w_in_b, w_out_b = w_in.astype(BF16), w_out.astype(BF16)
    w_gu_b, w_down_b = w_gate_up.astype(BF16), w_down.astype(BF16)
    x2 = x.reshape(n, d)
    for l in range(depth):
        q, k, v, u, gb = _in_proj(x2, vec(pre_mix_norm), w_in_b, cos, sin_up, sin_dn, l)
        attn = _attention(q.reshape(b, s, ATTN_WIDTH), k.reshape(b, s, ATTN_WIDTH),
                          v.reshape(b, s, ATTN_WIDTH))
        x2 = _mix(attn.reshape(n, ATTN_WIDTH), u, gb, x2, conv_w, vec(attn_out_norm),
                  vec(conv_out_norm), w_out_b, vec(post_mix_norm), l, s)
        x2 = _ffn(x2, vec(pre_ffn_norm), w_gu_b, w_down_b, vec(post_ffn_norm), l)
    return x2.reshape(b, s, d)
```

```python
import functools

import jax
import jax.numpy as jnp
from jax import lax
from jax.experimental import pallas as pl
from jax.experimental.pallas import tpu as pltpu

D_MODEL = 1024
HEAD_DIM = 64
ATTN_WIDTH = 512
CONV_WIDTH = 512
N_ATTN_HEADS = 8
IN_PROJ_WIDTH = 3 * ATTN_WIDTH + 3 * CONV_WIDTH
CONV_K = 3
ROPE_DIM = 16
ROPE_THETA = 500000.0
DILATIONS = (1, 4, 16)
HALF = 64
FFN_HIDDEN = 2816
RMS_EPS = 1e-6
NEG_INF = -1e30
LOG2_E = 1.4426950408889634

LANES = 128
ROW_TILE = 512
TQ = 128
NK = TQ + 2 * HALF
GROUP_ROWS = 8 * TQ
FFN_CHUNKS = ((0, 1536), (1536, 1280))

F32 = jnp.float32
BF16 = jnp.bfloat16


def _rms(x, g):
    return x * lax.rsqrt(jnp.mean(x * x, axis=-1, keepdims=True) + RMS_EPS) * g


def _in_proj_kernel(x_ref, g_ref, w_ref, c_ref, s1_ref, s2_ref,
                    q_ref, k_ref, v_ref, u_ref, gb_ref):
    h = _rms(x_ref[...], g_ref[...]).astype(BF16)

    def proj(c):
        return jnp.dot(h, w_ref[:, c * 512:(c + 1) * 512], preferred_element_type=F32)

    cos, sin_up, sin_dn = c_ref[...], s1_ref[...], s2_ref[...]

    def rotary(t):
        parts = []
        for j in range(ATTN_WIDTH // LANES):
            tj = t[:, j * LANES:(j + 1) * LANES]
            parts.append(tj * cos + pltpu.roll(tj, 8, 1) * sin_up
                         + pltpu.roll(tj, LANES - 8, 1) * sin_dn)
        return jnp.concatenate(parts, axis=1)

    q_ref[...] = (rotary(proj(0)) * (HEAD_DIM ** -0.5 * LOG2_E)).astype(BF16)
    k_ref[...] = rotary(proj(1)).astype(BF16)
    v_ref[...] = proj(2).astype(BF16)
    u_ref[...] = (proj(5) * proj(3)).astype(BF16)
    gb_ref[...] = proj(4).astype(BF16)


def _in_proj(x2, g, w, cos, sin_up, sin_dn, layer):
    n = x2.shape[0]
    row = lambda i: (i, 0)
    out = jax.ShapeDtypeStruct((n, 512), BF16)
    return pl.pallas_call(
        _in_proj_kernel,
        grid=(n // ROW_TILE,),
        in_specs=[
            pl.BlockSpec((ROW_TILE, D_MODEL), row),
            pl.BlockSpec((None, 1, D_MODEL), lambda i: (layer, 0, 0)),
            pl.BlockSpec((None, D_MODEL, IN_PROJ_WIDTH), lambda i: (layer, 0, 0),
                         pipeline_mode=pl.Buffered(1)),
            pl.BlockSpec((ROW_TILE, LANES), row),
            pl.BlockSpec((ROW_TILE, LANES), row),
            pl.BlockSpec((ROW_TILE, LANES), row),
        ],
        out_specs=[pl.BlockSpec((ROW_TILE, 512), row)] * 5,
        out_shape=[out] * 5,
        compiler_params=pltpu.CompilerParams(
            dimension_semantics=("parallel",), vmem_limit_bytes=40 * 2 ** 20),
        name="in_proj",
    )(x2, g, w, cos, sin_up, sin_dn)


def _branch_geometry(seq, d):
    length = seq // d
    if length <= 2 * TQ:
        return length, length, 0, 1
    return TQ, NK, -HALF, length // TQ


def _attn_kernel(q_ref, k_ref, v_ref, o_ref, qf, kf, vf, ob, lb, mb, bias, bias_whole, pbuf,
                 *, seq, pad):
    geometry = [_branch_geometry(seq, d) for d in DILATIONS]
    head_a = {tq: lax.broadcasted_iota(jnp.int32, (tq, LANES), 1) < HEAD_DIM
              for tq in {g[0] for g in geometry}}

    qf[...] = q_ref[0].astype(F32)
    zeros = jnp.zeros((pad, LANES), F32)
    for src, dst in ((k_ref, kf), (v_ref, vf)):
        dst[0:pad, :] = zeros
        dst[pad + seq:pad + seq + pad, :] = zeros
        dst[pad:pad + seq, :] = src[0].astype(F32)

    r = lax.broadcasted_iota(jnp.int32, (TQ, NK), 0)
    c = lax.broadcasted_iota(jnp.int32, (TQ, NK), 1)
    band = (c >= r) & (c <= r + 2 * HALF)
    bias[0] = jnp.where(band & (c >= HALF), 0.0, NEG_INF)
    bias[1] = jnp.where(band, 0.0, NEG_INF)
    bias[2] = jnp.where(band & (c < HALF + TQ), 0.0, NEG_INF)
    rw = lax.broadcasted_iota(jnp.int32, bias_whole.shape, 0)
    cw = lax.broadcasted_iota(jnp.int32, bias_whole.shape, 1)
    bias_whole[...] = jnp.where(jnp.abs(rw - cw) <= HALF, 0.0, NEG_INF)

    def rows(start, size, d):
        return pl.ds(start, size) if d == 1 else pl.ds(start, size, stride=d)

    n_groups = seq // GROUP_ROWS

    def group_blocks(br, t):
        tq, _, _, nblk = geometry[br]
        per_group = GROUP_ROWS // tq
        if nblk >= per_group:
            ngrp = nblk // per_group
            res, g = (t, 0) if ngrp == 1 else (t // ngrp, t % ngrp)
            static = isinstance(t, int) or ngrp == 1
            out = []
            for j in range(per_group):
                idx = 1
                if j == 0:
                    idx = (0 if g == 0 else 1) if static else jnp.where(g == 0, 0, 1)
                elif j == per_group - 1:
                    idx = (2 if g == ngrp - 1 else 1) if static else jnp.where(g == ngrp - 1, 2, 1)
                out.append((g * per_group + j, res, idx))
            return out
        per = per_group // nblk
        return [(i, t * per + jr, 0 if i == 0 else (2 if i == nblk - 1 else 1))
                for jr in range(per) for i in range(nblk)]

    def scores(br, t, slot):
        d = DILATIONS[br]
        tq, nk, koff, nblk = geometry[br]
        for n, (i, res, bias_idx) in enumerate(group_blocks(br, t)):
            q0 = res + d * tq * i
            k0 = pad + res + d * (tq * i + koff)
            qb = qf[rows(q0, tq, d), :]
            kb = kf[rows(k0, nk, d), :].astype(BF16)
            b = bias_whole[...] if nblk == 1 else bias[bias_idx]
            q2 = jnp.concatenate([jnp.where(head_a[tq], qb, 0.0), jnp.where(head_a[tq], 0.0, qb)],
                                 axis=0).astype(BF16)
            s2 = lax.dot_general(q2, kb, (((1,), (1,)), ((), ())), preferred_element_type=F32)
            ms = []
            for h in range(2):
                s = s2[h * tq:(h + 1) * tq] + b
                m = jnp.max(s, axis=1, keepdims=True)
                pbuf[slot, pl.ds((2 * n + h) * tq, tq), 0:nk] = jnp.exp2(s - m).astype(BF16)
                ms.append(m)
            mb[br, rows(q0, tq, d), :] = jnp.where(head_a[tq], ms[0], ms[1])

    def values(br, t, slot):
        d = DILATIONS[br]
        tq, nk, koff, _ = geometry[br]
        for n, (i, res, _) in enumerate(group_blocks(br, t)):
            q0 = res + d * tq * i
            k0 = pad + res + d * (tq * i + koff)
            vb = vf[rows(k0, nk, d), :].astype(BF16)
            r2 = jnp.dot(pbuf[slot, pl.ds(2 * n * tq, 2 * tq), 0:nk],
                         jnp.concatenate([vb, jnp.ones_like(vb)], axis=1),
                         preferred_element_type=F32)
            ob[br, rows(q0, tq, d), :] = jnp.where(head_a[tq], r2[:tq, :LANES], r2[tq:, :LANES])
            lb[br, rows(q0, tq, d), :] = jnp.where(head_a[tq], r2[:tq, LANES:], r2[tq:, LANES:])

    assert n_groups % 2 == 0
    scores(0, 0, 0)
    for br in range(len(DILATIONS)):
        def body(t, carry, br=br):
            values(br, t - 1, (t - 1) % 2)
            scores(br, t, t % 2)
            return carry

        lax.fori_loop(1, n_groups, body, 0)
        values(br, n_groups - 1, (n_groups - 1) % 2)
        if br + 1 < len(DILATIONS):
            scores(br + 1, 0, 0)

    chunk = 256

    def combine(j, carry):
        sl = pl.ds(pl.multiple_of(j * chunk, chunk), chunk)
        m0, m1, m2 = mb[0, sl, :], mb[1, sl, :], mb[2, sl, :]
        mx = jnp.maximum(jnp.maximum(m0, m1), m2)
        w = [jnp.exp2(m0 - mx), jnp.exp2(m1 - mx), jnp.exp2(m2 - mx)]
        num = sum(w[i] * ob[i, sl, :] for i in range(3))
        den = sum(w[i] * lb[i, sl, :] for i in range(3))
        o_ref[0, sl, :] = (num / den).astype(o_ref.dtype)
        return carry

    lax.fori_loop(0, seq // chunk, combine, 0)


def _attention(q, k, v):
    b, s, _ = q.shape
    geometry = [_branch_geometry(s, d) for d in DILATIONS]
    pad = max(-koff * d for d, (_, _, koff, _) in zip(DILATIONS, geometry))
    whole = max(tq for tq, _, _, nblk in geometry if nblk == 1)
    nk_max = max(nk for _, nk, _, _ in geometry)
    spec = pl.BlockSpec((1, s, LANES), lambda i, j: (i, 0, j))
    return pl.pallas_call(
        functools.partial(_attn_kernel, seq=s, pad=pad),
        grid=(b, ATTN_WIDTH // LANES),
        in_specs=[spec, spec, spec],
        out_specs=spec,
        out_shape=jax.ShapeDtypeStruct((b, s, ATTN_WIDTH), BF16),
        scratch_shapes=[
            pltpu.VMEM((s, LANES), F32),
            pltpu.VMEM((s + 2 * pad, LANES), F32),
            pltpu.VMEM((s + 2 * pad, LANES), F32),
            pltpu.VMEM((len(DILATIONS), s, LANES), F32),
            pltpu.VMEM((len(DILATIONS), s, LANES), F32),
            pltpu.VMEM((len(DILATIONS), s, LANES), F32),
            pltpu.VMEM((3, TQ, NK), F32),
            pltpu.VMEM((whole, whole), F32),
            pltpu.VMEM((2, 2 * GROUP_ROWS, nk_max), BF16),
        ],
        compiler_params=pltpu.CompilerParams(
            dimension_semantics=("parallel", "parallel"), vmem_limit_bytes=48 * 2 ** 20),
        name="attn",
    )(q, k, v)


def _mix_kernel(a_ref, u_ref, up_ref, un_ref, gb_ref, x_ref, cw_ref, ga_ref, gc_ref,
                w_ref, gp_ref, o_ref, *, tiles_per_seq):
    t = pl.program_id(0) % tiles_per_seq
    u = u_ref[...].astype(F32)
    tm = u.shape[0]
    prev_row = jnp.where(t == 0, 0.0, up_ref[7:8, :].astype(F32))
    next_row = jnp.where(t == tiles_per_seq - 1, 0.0, un_ref[0:1, :].astype(F32))
    row = lax.broadcasted_iota(jnp.int32, u.shape, 0)
    u_prev = jnp.where(row == 0, prev_row, pltpu.roll(u, 1, 0))
    u_next = jnp.where(row == tm - 1, next_row, pltpu.roll(u, tm - 1, 0))
    y = cw_ref[0:1, :] * u_prev + cw_ref[1:2, :] * u + cw_ref[2:3, :] * u_next
    conv = gb_ref[...].astype(F32) * y
    na = _rms(a_ref[...].astype(F32), ga_ref[...]).astype(BF16)
    nc = _rms(conv, gc_ref[...]).astype(BF16)
    mix = (jnp.dot(na, w_ref[0:ATTN_WIDTH, :], preferred_element_type=F32)
           + jnp.dot(nc, w_ref[ATTN_WIDTH:, :], preferred_element_type=F32))
    o_ref[...] = x_ref[...] + _rms(mix, gp_ref[...])


def _mix(attn2, u, gb, x2, conv_w, g_attn, g_conv, w_out, g_post, layer, seq):
    n = x2.shape[0]
    row = lambda i: (i, 0)
    halo = ROW_TILE // 8
    vec = lambda width: pl.BlockSpec((None, 1, width), lambda i: (layer, 0, 0))
    return pl.pallas_call(
        functools.partial(_mix_kernel, tiles_per_seq=seq // ROW_TILE),
        grid=(n // ROW_TILE,),
        in_specs=[
            pl.BlockSpec((ROW_TILE, ATTN_WIDTH), row),
            pl.BlockSpec((ROW_TILE, CONV_WIDTH), row),
            pl.BlockSpec((8, CONV_WIDTH), lambda i: (jnp.maximum(i * halo - 1, 0), 0)),
            pl.BlockSpec((8, CONV_WIDTH), lambda i: (jnp.minimum((i + 1) * halo, n // 8 - 1), 0)),
            pl.BlockSpec((ROW_TILE, CONV_WIDTH), row),
            pl.BlockSpec((ROW_TILE, D_MODEL), row),
            pl.BlockSpec((None, CONV_K, CONV_WIDTH), lambda i: (layer, 0, 0)),
            vec(ATTN_WIDTH),
            vec(CONV_WIDTH),
            pl.BlockSpec((None, D_MODEL, D_MODEL), lambda i: (layer, 0, 0),
                         pipeline_mode=pl.Buffered(1)),
            vec(D_MODEL),
        ],
        out_specs=pl.BlockSpec((ROW_TILE, D_MODEL), row),
        out_shape=jax.ShapeDtypeStruct((n, D_MODEL), F32),
        compiler_params=pltpu.CompilerParams(
            dimension_semantics=("parallel",), vmem_limit_bytes=40 * 2 ** 20),
        name="mix",
    )(attn2, u, u, u, gb, x2, conv_w, g_attn, g_conv, w_out, g_post)


def _ffn_kernel(x_ref, g1_ref, wgu_ref, wd_ref, g2_ref, o_ref):
    x = x_ref[...]
    h = _rms(x, g1_ref[...]).astype(BF16)
    f = None
    for start, size in FFN_CHUNKS:
        g = jnp.dot(h, wgu_ref[:, start:start + size], preferred_element_type=F32)
        u = jnp.dot(h, wgu_ref[:, FFN_HIDDEN + start:FFN_HIDDEN + start + size],
                    preferred_element_type=F32)
        act = (g * jax.nn.sigmoid(g) * u).astype(BF16)
        part = jnp.dot(act, wd_ref[start:start + size, :], preferred_element_type=F32)
        f = part if f is None else f + part
    o_ref[...] = x + _rms(f, g2_ref[...])


def _ffn(x2, g_pre, w_gu, w_down, g_post, layer):
    n = x2.shape[0]
    row = lambda i: (i, 0)
    vec = pl.BlockSpec((None, 1, D_MODEL), lambda i: (layer, 0, 0))
    return pl.pallas_call(
        _ffn_kernel,
        grid=(n // ROW_TILE,),
        in_specs=[
            pl.BlockSpec((ROW_TILE, D_MODEL), row),
            vec,
            pl.BlockSpec((None, D_MODEL, 2 * FFN_HIDDEN), lambda i: (layer, 0, 0),
                         pipeline_mode=pl.Buffered(1)),
            pl.BlockSpec((None, FFN_HIDDEN, D_MODEL), lambda i: (layer, 0, 0),
                         pipeline_mode=pl.Buffered(1)),
            vec,
        ],
        out_specs=pl.BlockSpec((ROW_TILE, D_MODEL), row),
        out_shape=jax.ShapeDtypeStruct((n, D_MODEL), F32),
        compiler_params=pltpu.CompilerParams(
            dimension_semantics=("parallel",), vmem_limit_bytes=56 * 2 ** 20),
        name="ffn",
    )(x2, g_pre, w_gu, w_down, g_post)


def _rotary_tables(positions):
    half = ROPE_DIM // 2
    inv_freq = ROPE_THETA ** (-jnp.arange(0, ROPE_DIM, 2, dtype=F32) / ROPE_DIM)
    in_head = jnp.arange(LANES) % HEAD_DIM
    freq = jnp.where(in_head < ROPE_DIM, inv_freq[in_head % half], 0.0)
    ang = positions.astype(F32).reshape(-1, 1) * freq
    cos, sin = jnp.cos(ang), jnp.sin(ang)
    s_up = jnp.where(in_head >= half, sin, 0.0)
    s_dn = jnp.where(in_head < half, -sin, 0.0)
    return cos, s_up, s_dn


def kernel(x, positions, pre_mix_norm, w_in, conv_w, attn_out_norm, conv_out_norm, w_out,
           post_mix_norm, pre_ffn_norm, w_gate_up, w_down, post_ffn_norm):
    b, s, d = x.shape
    depth = w_in.shape[0]
    n = b * s
    cos, sin_up, sin_dn = _rotary_tables(positions)
    vec = lambda g: g.reshape(depth, 1, g.shape[-1])
    w_in_b, w_out_b = w_in.astype(BF16), w_out.astype(BF16)
    w_gu_b, w_down_b = w_gate_up.astype(BF16), w_down.astype(BF16)
    x2 = x.reshape(n, d)
    for l in range(depth):
        q, k, v, u, gb = _in_proj(x2, vec(pre_mix_norm), w_in_b, cos, sin_up, sin_dn, l)
        attn = _attention(q.reshape(b, s, ATTN_WIDTH), k.reshape(b, s, ATTN_WIDTH),
                          v.reshape(b, s, ATTN_WIDTH))
        x2 = _mix(attn.reshape(n, ATTN_WIDTH), u, gb, x2, conv_w, vec(attn_out_norm),
                  vec(conv_out_norm), w_out_b, vec(post_mix_norm), l, s)
        x2 = _ffn(x2, vec(pre_ffn_norm), w_gu_b, w_down_b, vec(post_ffn_norm), l)
    return x2.reshape(b, s, d)
```

```python
import functools

import jax
import jax.numpy as jnp
from jax import lax
from jax.experimental import pallas as pl
from jax.experimental.pallas import tpu as pltpu

D_MODEL = 1024
HEAD_DIM = 64
ATTN_WIDTH = 512
CONV_WIDTH = 512
N_ATTN_HEADS = 8
IN_PROJ_WIDTH = 3 * ATTN_WIDTH + 3 * CONV_WIDTH
CONV_K = 3
ROPE_DIM = 16
ROPE_THETA = 500000.0
DILATIONS = (1, 4, 16)
HALF = 64
FFN_HIDDEN = 2816
RMS_EPS = 1e-6
NEG_INF = -1e30
LOG2_E = 1.4426950408889634

LANES = 128
ROW_TILE = 512
TQ = 128
NK = TQ + 2 * HALF
GROUP_ROWS = 8 * TQ
FFN_CHUNKS = ((0, 1536), (1536, 1280))

F32 = jnp.float32
BF16 = jnp.bfloat16


def _rms(x, g):
    return x * lax.rsqrt(jnp.mean(x * x, axis=-1, keepdims=True) + RMS_EPS) * g


def _in_proj_kernel(x_ref, g_ref, w_ref, c_ref, s1_ref, s2_ref,
                    q_ref, k_ref, v_ref, u_ref, gb_ref):
    h = _rms(x_ref[...], g_ref[...]).astype(BF16)

    def proj(c):
        return jnp.dot(h, w_ref[:, c * 512:(c + 1) * 512], preferred_element_type=F32)

    cos, sin_up, sin_dn = c_ref[...], s1_ref[...], s2_ref[...]

    def rotary(t):
        parts = []
        for j in range(ATTN_WIDTH // LANES):
            tj = t[:, j * LANES:(j + 1) * LANES]
            parts.append(tj * cos + pltpu.roll(tj, 8, 1) * sin_up
                         + pltpu.roll(tj, LANES - 8, 1) * sin_dn)
        return jnp.concatenate(parts, axis=1)

    q_ref[...] = (rotary(proj(0)) * (HEAD_DIM ** -0.5 * LOG2_E)).astype(BF16)
    k_ref[...] = rotary(proj(1)).astype(BF16)
    v_ref[...] = proj(2).astype(BF16)
    u_ref[...] = (proj(5) * proj(3)).astype(BF16)
    gb_ref[...] = proj(4).astype(BF16)


def _in_proj(x2, g, w, cos, sin_up, sin_dn, layer):
    n = x2.shape[0]
    row = lambda i: (i, 0)
    out = jax.ShapeDtypeStruct((n, 512), BF16)
    return pl.pallas_call(
        _in_proj_kernel,
        grid=(n // ROW_TILE,),
        in_specs=[
            pl.BlockSpec((ROW_TILE, D_MODEL), row),
            pl.BlockSpec((None, 1, D_MODEL), lambda i: (layer, 0, 0)),
            pl.BlockSpec((None, D_MODEL, IN_PROJ_WIDTH), lambda i: (layer, 0, 0),
                         pipeline_mode=pl.Buffered(1)),
            pl.BlockSpec((ROW_TILE, LANES), row),
            pl.BlockSpec((ROW_TILE, LANES), row),
            pl.BlockSpec((ROW_TILE, LANES), row),
        ],
        out_specs=[pl.BlockSpec((ROW_TILE, 512), row)] * 5,
        out_shape=[out] * 5,
        compiler_params=pltpu.CompilerParams(
            dimension_semantics=("parallel",), vmem_limit_bytes=40 * 2 ** 20),
        name="in_proj",
    )(x2, g, w, cos, sin_up, sin_dn)


def _branch_geometry(seq, d):
    length = seq // d
    if length <= 2 * TQ:
        return length, length, 0, 1
    return TQ, NK, -HALF, length // TQ


def _attn_kernel(q_ref, k_ref, v_ref, o_ref, qf, kf, vf, ob, lb, mb, bias, bias_whole, pbuf,
                 *, seq, pad):
    geometry = [_branch_geometry(seq, d) for d in DILATIONS]
    head_a = {tq: lax.broadcasted_iota(jnp.int32, (tq, LANES), 1) < HEAD_DIM
              for tq in {g[0] for g in geometry}}

    qf[...] = q_ref[0].astype(F32)
    zeros = jnp.zeros((pad, LANES), F32)
    for src, dst in ((k_ref, kf), (v_ref, vf)):
        dst[0:pad, :] = zeros
        dst[pad + seq:pad + seq + pad, :] = zeros
        dst[pad:pad + seq, :] = src[0].astype(F32)

    r = lax.broadcasted_iota(jnp.int32, (TQ, NK), 0)
    c = lax.broadcasted_iota(jnp.int32, (TQ, NK), 1)
    band = (c >= r) & (c <= r + 2 * HALF)
    bias[0] = jnp.where(band & (c >= HALF), 0.0, NEG_INF)
    bias[1] = jnp.where(band, 0.0, NEG_INF)
    bias[2] = jnp.where(band & (c < HALF + TQ), 0.0, NEG_INF)
    rw = lax.broadcasted_iota(jnp.int32, bias_whole.shape, 0)
    cw = lax.broadcasted_iota(jnp.int32, bias_whole.shape, 1)
    bias_whole[...] = jnp.where(jnp.abs(rw - cw) <= HALF, 0.0, NEG_INF)

    def rows(start, size, d):
        return pl.ds(start, size) if d == 1 else pl.ds(start, size, stride=d)

    n_groups = seq // GROUP_ROWS

    def group_blocks(br, t):
        tq, _, _, nblk = geometry[br]
        per_group = GROUP_ROWS // tq
        if nblk >= per_group:
            ngrp = nblk // per_group
            res, g = (t, 0) if ngrp == 1 else (t // ngrp, t % ngrp)
            static = isinstance(t, int) or ngrp == 1
            out = []
            for j in range(per_group):
                idx = 1
                if j == 0:
                    idx = (0 if g == 0 else 1) if static else jnp.where(g == 0, 0, 1)
                elif j == per_group - 1:
                    idx = (2 if g == ngrp - 1 else 1) if static else jnp.where(g == ngrp - 1, 2, 1)
                out.append((g * per_group + j, res, idx))
            return out
        per = per_group // nblk
        return [(i, t * per + jr, 0 if i == 0 else (2 if i == nblk - 1 else 1))
                for jr in range(per) for i in range(nblk)]

    def scores(br, t, slot):
        d = DILATIONS[br]
        tq, nk, koff, nblk = geometry[br]
        for n, (i, res, bias_idx) in enumerate(group_blocks(br, t)):
            q0 = res + d * tq * i
            k0 = pad + res + d * (tq * i + koff)
            qb = qf[rows(q0, tq, d), :]
            kb = kf[rows(k0, nk, d), :].astype(BF16)
            b = bias_whole[...] if nblk == 1 else bias[bias_idx]
            q2 = jnp.concatenate([jnp.where(head_a[tq], qb, 0.0), jnp.where(head_a[tq], 0.0, qb)],
                                 axis=0).astype(BF16)
            s2 = lax.dot_general(q2, kb, (((1,), (1,)), ((), ())), preferred_element_type=F32)
            ms = []
            for h in range(2):
                s = s2[h * tq:(h + 1) * tq] + b
                m = jnp.max(s, axis=1, keepdims=True)
                pbuf[slot, pl.ds((2 * n + h) * tq, tq), 0:nk] = jnp.exp2(s - m).astype(BF16)
                ms.append(m)
            mb[br, rows(q0, tq, d), :] = jnp.where(head_a[tq], ms[0], ms[1])

    def values(br, t, slot):
        d = DILATIONS[br]
        tq, nk, koff, _ = geometry[br]
        for n, (i, res, _) in enumerate(group_blocks(br, t)):
            q0 = res + d * tq * i
            k0 = pad + res + d * (tq * i + koff)
            vb = vf[rows(k0, nk, d), :].astype(BF16)
            r2 = jnp.dot(pbuf[slot, pl.ds(2 * n * tq, 2 * tq), 0:nk],
                         jnp.concatenate([vb, jnp.ones_like(vb)], axis=1),
                         preferred_element_type=F32)
            ob[br, rows(q0, tq, d), :] = jnp.where(head_a[tq], r2[:tq, :LANES], r2[tq:, :LANES])
            lb[br, rows(q0, tq, d), :] = jnp.where(head_a[tq], r2[:tq, LANES:], r2[tq:, LANES:])

    assert n_groups % 2 == 0
    scores(0, 0, 0)
    for br in range(len(DILATIONS)):
        def body(t, carry, br=br):
            values(br, t - 1, (t - 1) % 2)
            scores(br, t, t % 2)
            return carry

        lax.fori_loop(1, n_groups, body, 0)
        values(br, n_groups - 1, (n_groups - 1) % 2)
        if br + 1 < len(DILATIONS):
            scores(br + 1, 0, 0)

    chunk = 256

    def combine(j, carry):
        sl = pl.ds(pl.multiple_of(j * chunk, chunk), chunk)
        m0, m1, m2 = mb[0, sl, :], mb[1, sl, :], mb[2, sl, :]
        mx = jnp.maximum(jnp.maximum(m0, m1), m2)
        w = [jnp.exp2(m0 - mx), jnp.exp2(m1 - mx), jnp.exp2(m2 - mx)]
        num = sum(w[i] * ob[i, sl, :] for i in range(3))
        den = sum(w[i] * lb[i, sl, :] for i in range(3))
        o_ref[0, sl, :] = (num / den).astype(o_ref.dtype)
        return carry

    lax.fori_loop(0, seq // chunk, combine, 0)


def _attention(q, k, v):
    b, s, _ = q.shape
    geometry = [_branch_geometry(s, d) for d in DILATIONS]
    pad = max(-koff * d for d, (_, _, koff, _) in zip(DILATIONS, geometry))
    whole = max(tq for tq, _, _, nblk in geometry if nblk == 1)
    nk_max = max(nk for _, nk, _, _ in geometry)
    spec = pl.BlockSpec((1, s, LANES), lambda i, j: (i, 0, j))
    return pl.pallas_call(
        functools.partial(_attn_kernel, seq=s, pad=pad),
        grid=(b, ATTN_WIDTH // LANES),
        in_specs=[spec, spec, spec],
        out_specs=spec,
        out_shape=jax.ShapeDtypeStruct((b, s, ATTN_WIDTH), BF16),
        scratch_shapes=[
            pltpu.VMEM((s, LANES), F32),
            pltpu.VMEM((s + 2 * pad, LANES), F32),
            pltpu.VMEM((s + 2 * pad, LANES), F32),
            pltpu.VMEM((len(DILATIONS), s, LANES), F32),
            pltpu.VMEM((len(DILATIONS), s, LANES), F32),
            pltpu.VMEM((len(DILATIONS), s, LANES), F32),
            pltpu.VMEM((3, TQ, NK), F32),
            pltpu.VMEM((whole, whole), F32),
            pltpu.VMEM((2, 2 * GROUP_ROWS, nk_max), BF16),
        ],
        compiler_params=pltpu.CompilerParams(
            dimension_semantics=("parallel", "parallel"), vmem_limit_bytes=48 * 2 ** 20),
        name="attn",
    )(q, k, v)


def _post_kernel(a_ref, u_ref, up_ref, un_ref, gb_ref, x_ref, cw_ref, ga_ref, gc_ref, wo_ref,
                 gp_ref, g1_ref, wgu_ref, wd_ref, g2_ref, o_ref, *, tiles_per_seq):
    t = pl.program_id(0) % tiles_per_seq
    u = u_ref[...].astype(F32)
    tm = u.shape[0]
    prev_row = jnp.where(t == 0, 0.0, up_ref[7:8, :].astype(F32))
    next_row = jnp.where(t == tiles_per_seq - 1, 0.0, un_ref[0:1, :].astype(F32))
    row = lax.broadcasted_iota(jnp.int32, u.shape, 0)
    u_prev = jnp.where(row == 0, prev_row, pltpu.roll(u, 1, 0))
    u_next = jnp.where(row == tm - 1, next_row, pltpu.roll(u, tm - 1, 0))
    y = cw_ref[0:1, :] * u_prev + cw_ref[1:2, :] * u + cw_ref[2:3, :] * u_next
    conv = gb_ref[...].astype(F32) * y
    na = _rms(a_ref[...].astype(F32), ga_ref[...]).astype(BF16)
    nc = _rms(conv, gc_ref[...]).astype(BF16)
    mix = (jnp.dot(na, wo_ref[0:ATTN_WIDTH, :], preferred_element_type=F32)
           + jnp.dot(nc, wo_ref[ATTN_WIDTH:, :], preferred_element_type=F32))
    x = x_ref[...] + _rms(mix, gp_ref[...])
    h = _rms(x, g1_ref[...]).astype(BF16)
    f = None
    for start, size in FFN_CHUNKS:
        g = jnp.dot(h, wgu_ref[:, start:start + size], preferred_element_type=F32)
        up = jnp.dot(h, wgu_ref[:, FFN_HIDDEN + start:FFN_HIDDEN + start + size],
                     preferred_element_type=F32)
        act = (g * jax.nn.sigmoid(g) * up).astype(BF16)
        part = jnp.dot(act, wd_ref[start:start + size, :], preferred_element_type=F32)
        f = part if f is None else f + part
    o_ref[...] = x + _rms(f, g2_ref[...])


def _post(attn2, u, gb, x2, conv_w, g_attn, g_conv, w_out, g_post, g_pre, w_gu, w_down, g_post2,
          layer, seq):
    n = x2.shape[0]
    row = lambda i: (i, 0)
    halo = ROW_TILE // 8
    vec = lambda width: pl.BlockSpec((None, 1, width), lambda i: (layer, 0, 0))
    resident = lambda r, c: pl.BlockSpec((None, r, c), lambda i: (layer, 0, 0),
                                         pipeline_mode=pl.Buffered(1))
    return pl.pallas_call(
        functools.partial(_post_kernel, tiles_per_seq=seq // ROW_TILE),
        grid=(n // ROW_TILE,),
        in_specs=[
            pl.BlockSpec((ROW_TILE, ATTN_WIDTH), row),
            pl.BlockSpec((ROW_TILE, CONV_WIDTH), row),
            pl.BlockSpec((8, CONV_WIDTH), lambda i: (jnp.maximum(i * halo - 1, 0), 0)),
            pl.BlockSpec((8, CONV_WIDTH), lambda i: (jnp.minimum((i + 1) * halo, n // 8 - 1), 0)),
            pl.BlockSpec((ROW_TILE, CONV_WIDTH), row),
            pl.BlockSpec((ROW_TILE, D_MODEL), row),
            pl.BlockSpec((None, CONV_K, CONV_WIDTH), lambda i: (layer, 0, 0)),
            vec(ATTN_WIDTH),
            vec(CONV_WIDTH),
            resident(D_MODEL, D_MODEL),
            vec(D_MODEL),
            vec(D_MODEL),
            resident(D_MODEL, 2 * FFN_HIDDEN),
            resident(FFN_HIDDEN, D_MODEL),
            vec(D_MODEL),
        ],
        out_specs=pl.BlockSpec((ROW_TILE, D_MODEL), row),
        out_shape=jax.ShapeDtypeStruct((n, D_MODEL), F32),
        compiler_params=pltpu.CompilerParams(
            dimension_semantics=("parallel",), vmem_limit_bytes=58 * 2 ** 20),
        name="post",
    )(attn2, u, u, u, gb, x2, conv_w, g_attn, g_conv, w_out, g_post, g_pre, w_gu, w_down, g_post2)


def _rotary_tables(positions):
    half = ROPE_DIM // 2
    inv_freq = ROPE_THETA ** (-jnp.arange(0, ROPE_DIM, 2, dtype=F32) / ROPE_DIM)
    in_head = jnp.arange(LANES) % HEAD_DIM
    freq = jnp.where(in_head < ROPE_DIM, inv_freq[in_head % half], 0.0)
    ang = positions.astype(F32).reshape(-1, 1) * freq
    cos, sin = jnp.cos(ang), jnp.sin(ang)
    s_up = jnp.where(in_head >= half, sin, 0.0)
    s_dn = jnp.where(in_head < half, -sin, 0.0)
    return cos, s_up, s_dn


def kernel(x, positions, pre_mix_norm, w_in, conv_w, attn_out_norm, conv_out_norm, w_out,
           post_mix_norm, pre_ffn_norm, w_gate_up, w_down, post_ffn_norm):
    b, s, d = x.shape
    depth = w_in.shape[0]
    n = b * s
    cos, sin_up, sin_dn = _rotary_tables(positions)
    vec = lambda g: g.reshape(depth, 1, g.shape[-1])
    w_in_b, w_out_b = w_in.astype(BF16), w_out.astype(BF16)
    w_gu_b, w_down_b = w_gate_up.astype(BF16), w_down.astype(BF16)
    x2 = x.reshape(n, d)
    for l in range(depth):
        q, k, v, u, gb = _in_proj(x2, vec(pre_mix_norm), w_in_b, cos, sin_up, sin_dn, l)
        attn = _attention(q.reshape(b, s, ATTN_WIDTH), k.reshape(b, s, ATTN_WIDTH),
                          v.reshape(b, s, ATTN_WIDTH))
        x2 = _post(attn.reshape(n, ATTN_WIDTH), u, gb, x2, conv_w, vec(attn_out_norm),
                   vec(conv_out_norm), w_out_b, vec(post_mix_norm), vec(pre_ffn_norm), w_gu_b,
                   w_down_b, vec(post_ffn_norm), l, s)
    return x2.reshape(b, s, d)
```

```python
import functools

import jax
import jax.numpy as jnp
from jax import lax
from jax.experimental import pallas as pl
from jax.experimental.pallas import tpu as pltpu

D_MODEL = 1024
HEAD_DIM = 64
ATTN_WIDTH = 512
CONV_WIDTH = 512
N_ATTN_HEADS = 8
IN_PROJ_WIDTH = 3 * ATTN_WIDTH + 3 * CONV_WIDTH
CONV_K = 3
ROPE_DIM = 16
ROPE_THETA = 500000.0
DILATIONS = (1, 4, 16)
HALF = 64
FFN_HIDDEN = 2816
RMS_EPS = 1e-6
NEG_INF = -1e30
LOG2_E = 1.4426950408889634

LANES = 128
ROW_TILE = 1024
TQ = 128
NK = TQ + 2 * HALF
GROUP_ROWS = 8 * TQ
STAGE_CHUNK = 8
FFN_CHUNKS = ((0, 1536), (1536, 1280))
POST_TILE = 1024
POST_SLABS = 4

F32 = jnp.float32
BF16 = jnp.bfloat16


def _rms(x, g):
    return x * lax.rsqrt(jnp.mean(x * x, axis=-1, keepdims=True) + RMS_EPS) * g


def _in_proj_kernel(x_ref, g_ref, w_ref, c_ref, s1_ref, s2_ref,
                    q_ref, k_ref, v_ref, u_ref, gb_ref):
    h = _rms(x_ref[...], g_ref[...]).astype(BF16)

    def proj(c):
        return jnp.dot(h, w_ref[:, c * 512:(c + 1) * 512], preferred_element_type=F32)

    cos, sin_up, sin_dn = c_ref[...], s1_ref[...], s2_ref[...]

    def rotary(t):
        parts = []
        for j in range(ATTN_WIDTH // LANES):
            tj = t[:, j * LANES:(j + 1) * LANES]
            parts.append(tj * cos + pltpu.roll(tj, 8, 1) * sin_up
                         + pltpu.roll(tj, LANES - 8, 1) * sin_dn)
        return jnp.concatenate(parts, axis=1)

    q_ref[...] = (rotary(proj(0)) * (HEAD_DIM ** -0.5 * LOG2_E)).astype(BF16)
    k_ref[...] = rotary(proj(1)).astype(BF16)
    v_ref[...] = proj(2).astype(BF16)
    u_ref[...] = (proj(5) * proj(3)).astype(BF16)
    gb_ref[...] = proj(4).astype(BF16)


def _in_proj(x2, g, w, cos, sin_up, sin_dn, layer):
    n = x2.shape[0]
    row = lambda i: (i, 0)
    out = jax.ShapeDtypeStruct((n, 512), BF16)
    return pl.pallas_call(
        _in_proj_kernel,
        grid=(n // ROW_TILE,),
        in_specs=[
            pl.BlockSpec((ROW_TILE, D_MODEL), row),
            pl.BlockSpec((None, 1, D_MODEL), lambda i: (layer, 0, 0)),
            pl.BlockSpec((None, D_MODEL, IN_PROJ_WIDTH), lambda i: (layer, 0, 0),
                         pipeline_mode=pl.Buffered(1)),
            pl.BlockSpec((ROW_TILE, LANES), row),
            pl.BlockSpec((ROW_TILE, LANES), row),
            pl.BlockSpec((ROW_TILE, LANES), row),
        ],
        out_specs=[pl.BlockSpec((ROW_TILE, 512), row)] * 5,
        out_shape=[out] * 5,
        compiler_params=pltpu.CompilerParams(
            dimension_semantics=("parallel",), vmem_limit_bytes=40 * 2 ** 20),
        name="in_proj",
    )(x2, g, w, cos, sin_up, sin_dn)


def _branch_geometry(seq, d):
    length = seq // d
    if length <= 2 * TQ:
        return length, length, 0, 1
    return TQ, NK, -HALF, length // TQ


def _attn_kernel(q_ref, k_ref, v_ref, o_ref, qf, kf, vf, ob, lb, mb, bias, bias_whole, pbuf,
                 *, seq, pad):
    geometry = [_branch_geometry(seq, d) for d in DILATIONS]
    head_a = {tq: lax.broadcasted_iota(jnp.int32, (tq, LANES), 1) < HEAD_DIM
              for tq in {g[0] for g in geometry}}

    qf[...] = q_ref[0].astype(F32)
    zeros = jnp.zeros((pad, LANES), F32)
    for src, dst in ((k_ref, kf), (v_ref, vf)):
        dst[0:pad, :] = zeros
        dst[pad + seq:pad + seq + pad, :] = zeros
        dst[pad:pad + seq, :] = src[0].astype(F32)

    r = lax.broadcasted_iota(jnp.int32, (TQ, NK), 0)
    c = lax.broadcasted_iota(jnp.int32, (TQ, NK), 1)
    band = (c >= r) & (c <= r + 2 * HALF)
    bias[0] = jnp.where(band & (c >= HALF), 0.0, NEG_INF)
    bias[1] = jnp.where(band, 0.0, NEG_INF)
    bias[2] = jnp.where(band & (c < HALF + TQ), 0.0, NEG_INF)
    rw = lax.broadcasted_iota(jnp.int32, bias_whole.shape, 0)
    cw = lax.broadcasted_iota(jnp.int32, bias_whole.shape, 1)
    bias_whole[...] = jnp.where(jnp.abs(rw - cw) <= HALF, 0.0, NEG_INF)

    def rows(start, size, d):
        return pl.ds(start, size) if d == 1 else pl.ds(start, size, stride=d)

    n_groups = seq // GROUP_ROWS

    def group_blocks(br, t):
        tq, _, _, nblk = geometry[br]
        per_group = GROUP_ROWS // tq
        if nblk >= per_group:
            ngrp = nblk // per_group
            res, g = (t, 0) if ngrp == 1 else (t // ngrp, t % ngrp)
            static = isinstance(t, int) or ngrp == 1
            out = []
            for j in range(per_group):
                idx = 1
                if j == 0:
                    idx = (0 if g == 0 else 1) if static else jnp.where(g == 0, 0, 1)
                elif j == per_group - 1:
                    idx = (2 if g == ngrp - 1 else 1) if static else jnp.where(g == ngrp - 1, 2, 1)
                out.append((g * per_group + j, res, idx))
            return out
        per = per_group // nblk
        return [(i, t * per + jr, 0 if i == 0 else (2 if i == nblk - 1 else 1))
                for jr in range(per) for i in range(nblk)]

    def score_block(br, slot, n, blk):
        d = DILATIONS[br]
        tq, nk, koff, nblk = geometry[br]
        i, res, bias_idx = blk
        q0 = res + d * tq * i
        k0 = pad + res + d * (tq * i + koff)
        qb = qf[rows(q0, tq, d), :]
        kb = kf[rows(k0, nk, d), :].astype(BF16)
        b = bias_whole[...] if nblk == 1 else bias[bias_idx]
        q2 = jnp.concatenate([jnp.where(head_a[tq], qb, 0.0), jnp.where(head_a[tq], 0.0, qb)],
                             axis=0).astype(BF16)
        s2 = lax.dot_general(q2, kb, (((1,), (1,)), ((), ())), preferred_element_type=F32)
        ms = []
        for h in range(2):
            s = s2[h * tq:(h + 1) * tq] + b
            m = jnp.max(s, axis=1, keepdims=True)
            pbuf[slot, pl.ds((2 * n + h) * tq, tq), 0:nk] = jnp.exp2(s - m).astype(BF16)
            ms.append(m)
        mb[br, rows(q0, tq, d), :] = jnp.where(head_a[tq], ms[0], ms[1])

    def value_block(br, slot, n, blk):
        d = DILATIONS[br]
        tq, nk, koff, _ = geometry[br]
        i, res, _ = blk
        q0 = res + d * tq * i
        k0 = pad + res + d * (tq * i + koff)
        vb = vf[rows(k0, nk, d), :].astype(BF16)
        r2 = jnp.dot(pbuf[slot, pl.ds(2 * n * tq, 2 * tq), 0:nk],
                     jnp.concatenate([vb, jnp.ones_like(vb)], axis=1),
                     preferred_element_type=F32)
        ob[br, rows(q0, tq, d), :] = jnp.where(head_a[tq], r2[:tq, :LANES], r2[tq:, :LANES])
        lb[br, rows(q0, tq, d), :] = jnp.where(head_a[tq], r2[:tq, LANES:], r2[tq:, LANES:])

    def stage(prev, cur):
        v = [] if prev is None else [
            functools.partial(value_block, prev[0], prev[2], n, blk)
            for n, blk in enumerate(group_blocks(prev[0], prev[1]))]
        s = [] if cur is None else [
            functools.partial(score_block, cur[0], cur[2], n, blk)
            for n, blk in enumerate(group_blocks(cur[0], cur[1]))]
        for j in range(0, max(len(v), len(s)), STAGE_CHUNK):
            for thunks in (v, s):
                for thunk in thunks[j:j + STAGE_CHUNK]:
                    thunk()

    assert n_groups % 2 == 0
    last = n_groups - 1
    stage(None, (0, 0, 0))
    for br in range(len(DILATIONS)):
        def body(t, carry, br=br):
            stage((br, t - 1, (t - 1) % 2), (br, t, t % 2))
            return carry

        lax.fori_loop(1, n_groups, body, 0)
        stage((br, last, last % 2), (br + 1, 0, 0) if br + 1 < len(DILATIONS) else None)

    chunk = 256

    def combine(j, carry):
        sl = pl.ds(pl.multiple_of(j * chunk, chunk), chunk)
        m0, m1, m2 = mb[0, sl, :], mb[1, sl, :], mb[2, sl, :]
        mx = jnp.maximum(jnp.maximum(m0, m1), m2)
        w = [jnp.exp2(m0 - mx), jnp.exp2(m1 - mx), jnp.exp2(m2 - mx)]
        num = sum(w[i] * ob[i, sl, :] for i in range(3))
        den = sum(w[i] * lb[i, sl, :] for i in range(3))
        o_ref[0, sl, :] = (num / den).astype(o_ref.dtype)
        return carry

    lax.fori_loop(0, seq // chunk, combine, 0)


def _attention(q, k, v):
    b, s, _ = q.shape
    geometry = [_branch_geometry(s, d) for d in DILATIONS]
    pad = max(-koff * d for d, (_, _, koff, _) in zip(DILATIONS, geometry))
    whole = max(tq for tq, _, _, nblk in geometry if nblk == 1)
    nk_max = max(nk for _, nk, _, _ in geometry)
    spec = pl.BlockSpec((1, s, LANES), lambda i, j: (i, 0, j))
    return pl.pallas_call(
        functools.partial(_attn_kernel, seq=s, pad=pad),
        grid=(b, ATTN_WIDTH // LANES),
        in_specs=[spec, spec, spec],
        out_specs=spec,
        out_shape=jax.ShapeDtypeStruct((b, s, ATTN_WIDTH), BF16),
        scratch_shapes=[
            pltpu.VMEM((s, LANES), F32),
            pltpu.VMEM((s + 2 * pad, LANES), F32),
            pltpu.VMEM((s + 2 * pad, LANES), F32),
            pltpu.VMEM((len(DILATIONS), s, LANES), F32),
            pltpu.VMEM((len(DILATIONS), s, LANES), F32),
            pltpu.VMEM((len(DILATIONS), s, LANES), F32),
            pltpu.VMEM((3, TQ, NK), F32),
            pltpu.VMEM((whole, whole), F32),
            pltpu.VMEM((2, 2 * GROUP_ROWS, nk_max), BF16),
        ],
        compiler_params=pltpu.CompilerParams(
            dimension_semantics=("parallel", "parallel"), vmem_limit_bytes=48 * 2 ** 20),
        name="attn",
    )(q, k, v)


def _post_kernel(a_ref, u_ref, up_ref, un_ref, gb_ref, x_ref, cw_ref, ga_ref, gc_ref, wo_ref,
                 gp_ref, g1_ref, wgu_ref, wd_ref, g2_ref, o_ref, *, tiles_per_seq):
    t = pl.program_id(0) % tiles_per_seq
    u = u_ref[...].astype(F32)
    tm = u.shape[0]
    prev_row = jnp.where(t == 0, 0.0, up_ref[7:8, :].astype(F32))
    next_row = jnp.where(t == tiles_per_seq - 1, 0.0, un_ref[0:1, :].astype(F32))
    row = lax.broadcasted_iota(jnp.int32, u.shape, 0)
    u_prev = jnp.where(row == 0, prev_row, pltpu.roll(u, 1, 0))
    u_next = jnp.where(row == tm - 1, next_row, pltpu.roll(u, tm - 1, 0))
    y = cw_ref[0:1, :] * u_prev + cw_ref[1:2, :] * u + cw_ref[2:3, :] * u_next
    conv = gb_ref[...].astype(F32) * y
    slab = tm // POST_SLABS
    slabs = [slice(j * slab, (j + 1) * slab) for j in range(POST_SLABS)]
    xs = []
    for sl in slabs:
        na = _rms(a_ref[sl, :].astype(F32), ga_ref[...]).astype(BF16)
        nc = _rms(conv[sl], gc_ref[...]).astype(BF16)
        mix = (jnp.dot(na, wo_ref[0:ATTN_WIDTH, :], preferred_element_type=F32)
               + jnp.dot(nc, wo_ref[ATTN_WIDTH:, :], preferred_element_type=F32))
        xs.append(x_ref[sl, :] + _rms(mix, gp_ref[...]))
    for sl, x in zip(slabs, xs):
        h = _rms(x, g1_ref[...]).astype(BF16)
        f = None
        for start, size in FFN_CHUNKS:
            g = jnp.dot(h, wgu_ref[:, start:start + size], preferred_element_type=F32)
            up = jnp.dot(h, wgu_ref[:, FFN_HIDDEN + start:FFN_HIDDEN + start + size],
                         preferred_element_type=F32)
            act = (g * jax.nn.sigmoid(g) * up).astype(BF16)
            part = jnp.dot(act, wd_ref[start:start + size, :], preferred_element_type=F32)
            f = part if f is None else f + part
        o_ref[sl, :] = x + _rms(f, g2_ref[...])


def _post(attn2, u, gb, x2, conv_w, g_attn, g_conv, w_out, g_post, g_pre, w_gu, w_down, g_post2,
          layer, seq):
    n = x2.shape[0]
    row = lambda i: (i, 0)
    halo = POST_TILE // 8
    vec = lambda width: pl.BlockSpec((None, 1, width), lambda i: (layer, 0, 0))
    resident = lambda r, c: pl.BlockSpec((None, r, c), lambda i: (layer, 0, 0),
                                         pipeline_mode=pl.Buffered(1))
    return pl.pallas_call(
        functools.partial(_post_kernel, tiles_per_seq=seq // POST_TILE),
        grid=(n // POST_TILE,),
        in_specs=[
            pl.BlockSpec((POST_TILE, ATTN_WIDTH), row),
            pl.BlockSpec((POST_TILE, CONV_WIDTH), row),
            pl.BlockSpec((8, CONV_WIDTH), lambda i: (jnp.maximum(i * halo - 1, 0), 0)),
            pl.BlockSpec((8, CONV_WIDTH), lambda i: (jnp.minimum((i + 1) * halo, n // 8 - 1), 0)),
            pl.BlockSpec((POST_TILE, CONV_WIDTH), row),
            pl.BlockSpec((POST_TILE, D_MODEL), row),
            pl.BlockSpec((None, CONV_K, CONV_WIDTH), lambda i: (layer, 0, 0)),
            vec(ATTN_WIDTH),
            vec(CONV_WIDTH),
            resident(D_MODEL, D_MODEL),
            vec(D_MODEL),
            vec(D_MODEL),
            resident(D_MODEL, 2 * FFN_HIDDEN),
            resident(FFN_HIDDEN, D_MODEL),
            vec(D_MODEL),
        ],
        out_specs=pl.BlockSpec((POST_TILE, D_MODEL), row),
        out_shape=jax.ShapeDtypeStruct((n, D_MODEL), F32),
        compiler_params=pltpu.CompilerParams(
            dimension_semantics=("parallel",), vmem_limit_bytes=58 * 2 ** 20),
        name="post",
    )(attn2, u, u, u, gb, x2, conv_w, g_attn, g_conv, w_out, g_post, g_pre, w_gu, w_down, g_post2)


def _rotary_tables(positions):
    half = ROPE_DIM // 2
    inv_freq = ROPE_THETA ** (-jnp.arange(0, ROPE_DIM, 2, dtype=F32) / ROPE_DIM)
    in_head = jnp.arange(LANES) % HEAD_DIM
    freq = jnp.where(in_head < ROPE_DIM, inv_freq[in_head % half], 0.0)
    ang = positions.astype(F32).reshape(-1, 1) * freq
    cos, sin = jnp.cos(ang), jnp.sin(ang)
    s_up = jnp.where(in_head >= half, sin, 0.0)
    s_dn = jnp.where(in_head < half, -sin, 0.0)
    return cos, s_up, s_dn


def kernel(x, positions, pre_mix_norm, w_in, conv_w, attn_out_norm, conv_out_norm, w_out,
           post_mix_norm, pre_ffn_norm, w_gate_up, w_down, post_ffn_norm):
    b, s, d = x.shape
    depth = w_in.shape[0]
    n = b * s
    cos, sin_up, sin_dn = _rotary_tables(positions)
    vec = lambda g: g.reshape(depth, 1, g.shape[-1])
    w_in_b, w_out_b = w_in.astype(BF16), w_out.astype(BF16)
    w_gu_b, w_down_b = w_gate_up.astype(BF16), w_down.astype(BF16)
    x2 = x.reshape(n, d)
    for l in range(depth):
        q, k, v, u, gb = _in_proj(x2, vec(pre_mix_norm), w_in_b, cos, sin_up, sin_dn, l)
        attn = _attention(q.reshape(b, s, ATTN_WIDTH), k.reshape(b, s, ATTN_WIDTH),
                          v.reshape(b, s, ATTN_WIDTH))
        x2 = _post(attn.reshape(n, ATTN_WIDTH), u, gb, x2, conv_w, vec(attn_out_norm),
                   vec(conv_out_norm), w_out_b, vec(post_mix_norm), vec(pre_ffn_norm), w_gu_b,
                   w_down_b, vec(post_ffn_norm), l, s)
    return x2.reshape(b, s, d)
```

```python
import functools

import jax
import jax.numpy as jnp
from jax import lax
from jax.experimental import pallas as pl
from jax.experimental.pallas import tpu as pltpu

D_MODEL = 1024
HEAD_DIM = 64
ATTN_WIDTH = 512
CONV_WIDTH = 512
N_ATTN_HEADS = 8
IN_PROJ_WIDTH = 3 * ATTN_WIDTH + 3 * CONV_WIDTH
CONV_K = 3
ROPE_DIM = 16
ROPE_THETA = 500000.0
DILATIONS = (1, 4, 16)
HALF = 64
FFN_HIDDEN = 2816
RMS_EPS = 1e-6
NEG_INF = -1e30
LOG2_E = 1.4426950408889634

LANES = 128
ROW_TILE = 1024
TQ = 128
NK = TQ + 2 * HALF
GROUP_ROWS = 8 * TQ
STAGE_CHUNK = 8
FFN_CHUNKS = ((0, 1536), (1536, 1280))
POST_TILE = 1024
POST_SLABS = 4

F32 = jnp.float32
BF16 = jnp.bfloat16


def _rms(x, g):
    return x * lax.rsqrt(jnp.mean(x * x, axis=-1, keepdims=True) + RMS_EPS) * g


def _in_proj_kernel(x_ref, g_ref, w_ref, c_ref, s1_ref, s2_ref, *refs):
    n_cast = (len(refs) - 5) // 2
    cast_src = refs[:n_cast]
    q_ref, k_ref, v_ref, u_ref, gb_ref = refs[n_cast:n_cast + 5]
    cast_dst = refs[n_cast + 5:]
    h = _rms(x_ref[...], g_ref[...]).astype(BF16)

    def proj(c):
        return jnp.dot(h, w_ref[:, c * 512:(c + 1) * 512], preferred_element_type=F32)

    cos, sin_up, sin_dn = c_ref[...], s1_ref[...], s2_ref[...]

    def rotary(t):
        parts = []
        for j in range(ATTN_WIDTH // LANES):
            tj = t[:, j * LANES:(j + 1) * LANES]
            parts.append(tj * cos + pltpu.roll(tj, 8, 1) * sin_up
                         + pltpu.roll(tj, LANES - 8, 1) * sin_dn)
        return jnp.concatenate(parts, axis=1)

    q_ref[...] = (rotary(proj(0)) * (HEAD_DIM ** -0.5 * LOG2_E)).astype(BF16)
    k_ref[...] = rotary(proj(1)).astype(BF16)
    v_ref[...] = proj(2).astype(BF16)
    u_ref[...] = (proj(5) * proj(3)).astype(BF16)
    gb_ref[...] = proj(4).astype(BF16)
    for src, dst in zip(cast_src, cast_dst):
        dst[...] = src[...].astype(BF16)


def _in_proj(x2, g, w, cos, sin_up, sin_dn, layer, casts):
    n = x2.shape[0]
    steps = n // ROW_TILE
    row = lambda i: (i, 0)
    out = jax.ShapeDtypeStruct((n, 512), BF16)
    cast_in_specs, cast_out_specs, cast_shapes = [], [], []
    for wt, lyr in casts:
        r, c = wt.shape[1:]
        cast_in_specs.append(pl.BlockSpec((None, r // steps, c), lambda i, lyr=lyr: (lyr, i, 0)))
        cast_out_specs.append(pl.BlockSpec((r // steps, c), row))
        cast_shapes.append(jax.ShapeDtypeStruct((r, c), BF16))
    res = pl.pallas_call(
        _in_proj_kernel,
        grid=(steps,),
        in_specs=[
            pl.BlockSpec((ROW_TILE, D_MODEL), row),
            pl.BlockSpec((None, 1, D_MODEL), lambda i: (layer, 0, 0)),
            pl.BlockSpec((D_MODEL, IN_PROJ_WIDTH), lambda i: (0, 0), pipeline_mode=pl.Buffered(1)),
            pl.BlockSpec((ROW_TILE, LANES), row),
            pl.BlockSpec((ROW_TILE, LANES), row),
            pl.BlockSpec((ROW_TILE, LANES), row),
        ] + cast_in_specs,
        out_specs=[pl.BlockSpec((ROW_TILE, 512), row)] * 5 + cast_out_specs,
        out_shape=[out] * 5 + cast_shapes,
        compiler_params=pltpu.CompilerParams(
            dimension_semantics=("parallel",), vmem_limit_bytes=48 * 2 ** 20),
        name="in_proj",
    )(x2, g, w, cos, sin_up, sin_dn, *[wt for wt, _ in casts])
    return res[:5], res[5:]


def _branch_geometry(seq, d):
    length = seq // d
    if length <= 2 * TQ:
        return length, length, 0, 1
    return TQ, NK, -HALF, length // TQ


def _attn_kernel(q_ref, k_ref, v_ref, o_ref, qf, kf, vf, ob, lb, mb, bias, bias_whole, pbuf,
                 *, seq, pad):
    geometry = [_branch_geometry(seq, d) for d in DILATIONS]
    head_a = {tq: lax.broadcasted_iota(jnp.int32, (tq, LANES), 1) < HEAD_DIM
              for tq in {g[0] for g in geometry}}

    qf[...] = q_ref[0].astype(F32)
    zeros = jnp.zeros((pad, LANES), F32)
    for src, dst in ((k_ref, kf), (v_ref, vf)):
        dst[0:pad, :] = zeros
        dst[pad + seq:pad + seq + pad, :] = zeros
        dst[pad:pad + seq, :] = src[0].astype(F32)

    r = lax.broadcasted_iota(jnp.int32, (TQ, NK), 0)
    c = lax.broadcasted_iota(jnp.int32, (TQ, NK), 1)
    band = (c >= r) & (c <= r + 2 * HALF)
    bias[0] = jnp.where(band & (c >= HALF), 0.0, NEG_INF)
    bias[1] = jnp.where(band, 0.0, NEG_INF)
    bias[2] = jnp.where(band & (c < HALF + TQ), 0.0, NEG_INF)
    rw = lax.broadcasted_iota(jnp.int32, bias_whole.shape, 0)
    cw = lax.broadcasted_iota(jnp.int32, bias_whole.shape, 1)
    bias_whole[...] = jnp.where(jnp.abs(rw - cw) <= HALF, 0.0, NEG_INF)

    def rows(start, size, d):
        return pl.ds(start, size) if d == 1 else pl.ds(start, size, stride=d)

    n_groups = seq // GROUP_ROWS

    def group_blocks(br, t):
        tq, _, _, nblk = geometry[br]
        per_group = GROUP_ROWS // tq
        if nblk >= per_group:
            ngrp = nblk // per_group
            res, g = (t, 0) if ngrp == 1 else (t // ngrp, t % ngrp)
            static = isinstance(t, int) or ngrp == 1
            out = []
            for j in range(per_group):
                idx = 1
                if j == 0:
                    idx = (0 if g == 0 else 1) if static else jnp.where(g == 0, 0, 1)
                elif j == per_group - 1:
                    idx = (2 if g == ngrp - 1 else 1) if static else jnp.where(g == ngrp - 1, 2, 1)
                out.append((g * per_group + j, res, idx))
            return out
        per = per_group // nblk
        return [(i, t * per + jr, 0 if i == 0 else (2 if i == nblk - 1 else 1))
                for jr in range(per) for i in range(nblk)]

    def score_block(br, slot, n, blk):
        d = DILATIONS[br]
        tq, nk, koff, nblk = geometry[br]
        i, res, bias_idx = blk
        q0 = res + d * tq * i
        k0 = pad + res + d * (tq * i + koff)
        qb = qf[rows(q0, tq, d), :]
        kb = kf[rows(k0, nk, d), :].astype(BF16)
        b = bias_whole[...] if nblk == 1 else bias[bias_idx]
        q2 = jnp.concatenate([jnp.where(head_a[tq], qb, 0.0), jnp.where(head_a[tq], 0.0, qb)],
                             axis=0).astype(BF16)
        s2 = lax.dot_general(q2, kb, (((1,), (1,)), ((), ())), preferred_element_type=F32)
        ms = []
        for h in range(2):
            s = s2[h * tq:(h + 1) * tq] + b
            m = jnp.max(s, axis=1, keepdims=True)
            pbuf[slot, pl.ds((2 * n + h) * tq, tq), 0:nk] = jnp.exp2(s - m).astype(BF16)
            ms.append(m)
        mb[br, rows(q0, tq, d), :] = jnp.where(head_a[tq], ms[0], ms[1])

    def value_block(br, slot, n, blk):
        d = DILATIONS[br]
        tq, nk, koff, _ = geometry[br]
        i, res, _ = blk
        q0 = res + d * tq * i
        k0 = pad + res + d * (tq * i + koff)
        vb = vf[rows(k0, nk, d), :].astype(BF16)
        r2 = jnp.dot(pbuf[slot, pl.ds(2 * n * tq, 2 * tq), 0:nk],
                     jnp.concatenate([vb, jnp.ones_like(vb)], axis=1),
                     preferred_element_type=F32)
        ob[br, rows(q0, tq, d), :] = jnp.where(head_a[tq], r2[:tq, :LANES], r2[tq:, :LANES])
        lb[br, rows(q0, tq, d), :] = jnp.where(head_a[tq], r2[:tq, LANES:], r2[tq:, LANES:])

    def stage(prev, cur):
        v = [] if prev is None else [
            functools.partial(value_block, prev[0], prev[2], n, blk)
            for n, blk in enumerate(group_blocks(prev[0], prev[1]))]
        s = [] if cur is None else [
            functools.partial(score_block, cur[0], cur[2], n, blk)
            for n, blk in enumerate(group_blocks(cur[0], cur[1]))]
        for j in range(0, max(len(v), len(s)), STAGE_CHUNK):
            for thunks in (v, s):
                for thunk in thunks[j:j + STAGE_CHUNK]:
                    thunk()

    assert n_groups % 2 == 0
    last = n_groups - 1
    stage(None, (0, 0, 0))
    for br in range(len(DILATIONS)):
        def body(t, carry, br=br):
            stage((br, t - 1, (t - 1) % 2), (br, t, t % 2))
            return carry

        lax.fori_loop(1, n_groups, body, 0)
        stage((br, last, last % 2), (br + 1, 0, 0) if br + 1 < len(DILATIONS) else None)

    chunk = 256

    def combine(j, carry):
        sl = pl.ds(pl.multiple_of(j * chunk, chunk), chunk)
        m0, m1, m2 = mb[0, sl, :], mb[1, sl, :], mb[2, sl, :]
        mx = jnp.maximum(jnp.maximum(m0, m1), m2)
        w = [jnp.exp2(m0 - mx), jnp.exp2(m1 - mx), jnp.exp2(m2 - mx)]
        num = sum(w[i] * ob[i, sl, :] for i in range(3))
        den = sum(w[i] * lb[i, sl, :] for i in range(3))
        o_ref[0, sl, :] = (num / den).astype(o_ref.dtype)
        return carry

    lax.fori_loop(0, seq // chunk, combine, 0)


def _attention(q, k, v):
    b, s, _ = q.shape
    geometry = [_branch_geometry(s, d) for d in DILATIONS]
    pad = max(-koff * d for d, (_, _, koff, _) in zip(DILATIONS, geometry))
    whole = max(tq for tq, _, _, nblk in geometry if nblk == 1)
    nk_max = max(nk for _, nk, _, _ in geometry)
    spec = pl.BlockSpec((1, s, LANES), lambda i, j: (i, 0, j))
    return pl.pallas_call(
        functools.partial(_attn_kernel, seq=s, pad=pad),
        grid=(b, ATTN_WIDTH // LANES),
        in_specs=[spec, spec, spec],
        out_specs=spec,
        out_shape=jax.ShapeDtypeStruct((b, s, ATTN_WIDTH), BF16),
        scratch_shapes=[
            pltpu.VMEM((s, LANES), F32),
            pltpu.VMEM((s + 2 * pad, LANES), F32),
            pltpu.VMEM((s + 2 * pad, LANES), F32),
            pltpu.VMEM((len(DILATIONS), s, LANES), F32),
            pltpu.VMEM((len(DILATIONS), s, LANES), F32),
            pltpu.VMEM((len(DILATIONS), s, LANES), F32),
            pltpu.VMEM((3, TQ, NK), F32),
            pltpu.VMEM((whole, whole), F32),
            pltpu.VMEM((2, 2 * GROUP_ROWS, nk_max), BF16),
        ],
        compiler_params=pltpu.CompilerParams(
            dimension_semantics=("parallel", "parallel"), vmem_limit_bytes=48 * 2 ** 20),
        name="attn",
    )(q, k, v)


def _post_kernel(a_ref, u_ref, up_ref, un_ref, gb_ref, x_ref, cw_ref, ga_ref, gc_ref, wo_ref,
                 gp_ref, g1_ref, wgu_ref, wd_ref, g2_ref, o_ref, *, tiles_per_seq):
    t = pl.program_id(0) % tiles_per_seq
    u = u_ref[...].astype(F32)
    tm = u.shape[0]
    prev_row = jnp.where(t == 0, 0.0, up_ref[7:8, :].astype(F32))
    next_row = jnp.where(t == tiles_per_seq - 1, 0.0, un_ref[0:1, :].astype(F32))
    row = lax.broadcasted_iota(jnp.int32, u.shape, 0)
    u_prev = jnp.where(row == 0, prev_row, pltpu.roll(u, 1, 0))
    u_next = jnp.where(row == tm - 1, next_row, pltpu.roll(u, tm - 1, 0))
    y = cw_ref[0:1, :] * u_prev + cw_ref[1:2, :] * u + cw_ref[2:3, :] * u_next
    conv = gb_ref[...].astype(F32) * y
    slab = tm // POST_SLABS
    slabs = [slice(j * slab, (j + 1) * slab) for j in range(POST_SLABS)]
    xs = []
    for sl in slabs:
        na = _rms(a_ref[sl, :].astype(F32), ga_ref[...]).astype(BF16)
        nc = _rms(conv[sl], gc_ref[...]).astype(BF16)
        mix = (jnp.dot(na, wo_ref[0:ATTN_WIDTH, :], preferred_element_type=F32)
               + jnp.dot(nc, wo_ref[ATTN_WIDTH:, :], preferred_element_type=F32))
        xs.append(x_ref[sl, :] + _rms(mix, gp_ref[...]))
    for sl, x in zip(slabs, xs):
        h = _rms(x, g1_ref[...]).astype(BF16)
        f = None
        for start, size in FFN_CHUNKS:
            g = jnp.dot(h, wgu_ref[:, start:start + size], preferred_element_type=F32)
            up = jnp.dot(h, wgu_ref[:, FFN_HIDDEN + start:FFN_HIDDEN + start + size],
                         preferred_element_type=F32)
            act = (g * jax.nn.sigmoid(g) * up).astype(BF16)
            part = jnp.dot(act, wd_ref[start:start + size, :], preferred_element_type=F32)
            f = part if f is None else f + part
        o_ref[sl, :] = x + _rms(f, g2_ref[...])


def _post(attn2, u, gb, x2, conv_w, g_attn, g_conv, w_out, g_post, g_pre, w_gu, w_down, g_post2,
          layer, seq):
    n = x2.shape[0]
    row = lambda i: (i, 0)
    halo = POST_TILE // 8
    vec = lambda width: pl.BlockSpec((None, 1, width), lambda i: (layer, 0, 0))
    resident = lambda r, c: pl.BlockSpec((r, c), lambda i: (0, 0), pipeline_mode=pl.Buffered(1))
    return pl.pallas_call(
        functools.partial(_post_kernel, tiles_per_seq=seq // POST_TILE),
        grid=(n // POST_TILE,),
        in_specs=[
            pl.BlockSpec((POST_TILE, ATTN_WIDTH), row),
            pl.BlockSpec((POST_TILE, CONV_WIDTH), row),
            pl.BlockSpec((8, CONV_WIDTH), lambda i: (jnp.maximum(i * halo - 1, 0), 0)),
            pl.BlockSpec((8, CONV_WIDTH), lambda i: (jnp.minimum((i + 1) * halo, n // 8 - 1), 0)),
            pl.BlockSpec((POST_TILE, CONV_WIDTH), row),
            pl.BlockSpec((POST_TILE, D_MODEL), row),
            pl.BlockSpec((None, CONV_K, CONV_WIDTH), lambda i: (layer, 0, 0)),
            vec(ATTN_WIDTH),
            vec(CONV_WIDTH),
            resident(D_MODEL, D_MODEL),
            vec(D_MODEL),
            vec(D_MODEL),
            resident(D_MODEL, 2 * FFN_HIDDEN),
            resident(FFN_HIDDEN, D_MODEL),
            vec(D_MODEL),
        ],
        out_specs=pl.BlockSpec((POST_TILE, D_MODEL), row),
        out_shape=jax.ShapeDtypeStruct((n, D_MODEL), F32),
        compiler_params=pltpu.CompilerParams(
            dimension_semantics=("parallel",), vmem_limit_bytes=58 * 2 ** 20),
        name="post",
    )(attn2, u, u, u, gb, x2, conv_w, g_attn, g_conv, w_out, g_post, g_pre, w_gu, w_down, g_post2)


def _rotary_tables(positions):
    half = ROPE_DIM // 2
    inv_freq = ROPE_THETA ** (-jnp.arange(0, ROPE_DIM, 2, dtype=F32) / ROPE_DIM)
    in_head = jnp.arange(LANES) % HEAD_DIM
    ang = positions.astype(F32).reshape(-1, 1) * inv_freq
    cos = jnp.tile(jnp.cos(ang), (1, LANES // half))
    sin = jnp.tile(jnp.sin(ang), (1, LANES // half))
    c = jnp.where(in_head < ROPE_DIM, cos, 1.0)
    s_up = jnp.where((in_head >= half) & (in_head < ROPE_DIM), sin, 0.0)
    s_dn = jnp.where(in_head < half, -sin, 0.0)
    return c, s_up, s_dn


def kernel(x, positions, pre_mix_norm, w_in, conv_w, attn_out_norm, conv_out_norm, w_out,
           post_mix_norm, pre_ffn_norm, w_gate_up, w_down, post_ffn_norm):
    b, s, d = x.shape
    depth = w_in.shape[0]
    n = b * s
    cos, sin_up, sin_dn = _rotary_tables(positions)
    vec = lambda g: g.reshape(depth, 1, g.shape[-1])
    w_in_b = w_in[0].astype(BF16)
    x2 = x.reshape(n, d)
    for l in range(depth):
        casts = [(w_out, l), (w_gate_up, l), (w_down, l)] + ([(w_in, l + 1)] if l + 1 < depth else [])
        (q, k, v, u, gb), cast = _in_proj(x2, vec(pre_mix_norm), w_in_b, cos, sin_up, sin_dn, l,
                                          casts)
        w_out_b, w_gu_b, w_down_b = cast[:3]
        w_in_b = cast[3] if l + 1 < depth else None
        attn = _attention(q.reshape(b, s, ATTN_WIDTH), k.reshape(b, s, ATTN_WIDTH),
                          v.reshape(b, s, ATTN_WIDTH))
        x2 = _post(attn.reshape(n, ATTN_WIDTH), u, gb, x2, conv_w, vec(attn_out_norm),
                   vec(conv_out_norm), w_out_b, vec(post_mix_norm), vec(pre_ffn_norm), w_gu_b,
                   w_down_b, vec(post_ffn_norm), l, s)
    return x2.reshape(b, s, d)
```

```python
import functools

import jax
import jax.numpy as jnp
from jax import lax
from jax.experimental import pallas as pl
from jax.experimental.pallas import tpu as pltpu

D_MODEL = 1024
HEAD_DIM = 64
ATTN_WIDTH = 512
CONV_WIDTH = 512
N_ATTN_HEADS = 8
IN_PROJ_WIDTH = 3 * ATTN_WIDTH + 3 * CONV_WIDTH
CONV_K = 3
ROPE_DIM = 16
ROPE_THETA = 500000.0
DILATIONS = (1, 4, 16)
HALF = 64
FFN_HIDDEN = 2816
RMS_EPS = 1e-6
NEG_INF = -1e30
LOG2_E = 1.4426950408889634

LANES = 128
ROW_TILE = 1024
TQ = 128
NK = TQ + 2 * HALF
GROUP_ROWS = 8 * TQ
STAGE_CHUNK = 8
FFN_CHUNKS = ((0, 1536), (1536, 1280))
POST_TILE = 1024
POST_SLABS = 4
POST_ORDER = (tuple(("mix", j) for j in range(POST_SLABS))
              + tuple((c, j) for j in range(POST_SLABS) for c in range(len(FFN_CHUNKS))))

F32 = jnp.float32
BF16 = jnp.bfloat16


def _rms(x, g):
    return x * lax.rsqrt(jnp.mean(x * x, axis=-1, keepdims=True) + RMS_EPS) * g


def _in_proj_kernel(x_ref, g_ref, w_ref, c_ref, s1_ref, s2_ref, *refs):
    n_cast = (len(refs) - 5) // 2
    cast_src = refs[:n_cast]
    q_ref, k_ref, v_ref, u_ref, gb_ref = refs[n_cast:n_cast + 5]
    cast_dst = refs[n_cast + 5:]
    h = _rms(x_ref[...], g_ref[...]).astype(BF16)

    def proj(c):
        return jnp.dot(h, w_ref[:, c * 512:(c + 1) * 512], preferred_element_type=F32)

    cos, sin_up, sin_dn = c_ref[...], s1_ref[...], s2_ref[...]

    def rotary(t):
        parts = []
        for j in range(ATTN_WIDTH // LANES):
            tj = t[:, j * LANES:(j + 1) * LANES]
            parts.append(tj * cos + pltpu.roll(tj, 8, 1) * sin_up
                         + pltpu.roll(tj, LANES - 8, 1) * sin_dn)
        return jnp.concatenate(parts, axis=1)

    q_ref[...] = (rotary(proj(0)) * (HEAD_DIM ** -0.5 * LOG2_E)).astype(BF16)
    k_ref[...] = rotary(proj(1)).astype(BF16)
    v_ref[...] = proj(2).astype(BF16)
    u_ref[...] = (proj(5) * proj(3)).astype(BF16)
    gb_ref[...] = proj(4).astype(BF16)
    for src, dst in zip(cast_src, cast_dst):
        dst[...] = src[...].astype(BF16)


def _in_proj(x2, g, w, cos, sin_up, sin_dn, layer, casts):
    n = x2.shape[0]
    steps = n // ROW_TILE
    row = lambda i: (i, 0)
    out = jax.ShapeDtypeStruct((n, 512), BF16)
    cast_in_specs, cast_out_specs, cast_shapes = [], [], []
    for wt, lyr in casts:
        r, c = wt.shape[1:]
        cast_in_specs.append(pl.BlockSpec((None, r // steps, c), lambda i, lyr=lyr: (lyr, i, 0)))
        cast_out_specs.append(pl.BlockSpec((r // steps, c), row))
        cast_shapes.append(jax.ShapeDtypeStruct((r, c), BF16))
    res = pl.pallas_call(
        _in_proj_kernel,
        grid=(steps,),
        in_specs=[
            pl.BlockSpec((ROW_TILE, D_MODEL), row),
            pl.BlockSpec((None, 1, D_MODEL), lambda i: (layer, 0, 0)),
            pl.BlockSpec((D_MODEL, IN_PROJ_WIDTH), lambda i: (0, 0), pipeline_mode=pl.Buffered(1)),
            pl.BlockSpec((ROW_TILE, LANES), row),
            pl.BlockSpec((ROW_TILE, LANES), row),
            pl.BlockSpec((ROW_TILE, LANES), row),
        ] + cast_in_specs,
        out_specs=[pl.BlockSpec((ROW_TILE, 512), row)] * 5 + cast_out_specs,
        out_shape=[out] * 5 + cast_shapes,
        compiler_params=pltpu.CompilerParams(
            dimension_semantics=("parallel",), vmem_limit_bytes=48 * 2 ** 20),
        name="in_proj",
    )(x2, g, w, cos, sin_up, sin_dn, *[wt for wt, _ in casts])
    return res[:5], res[5:]


def _branch_geometry(seq, d):
    length = seq // d
    if length <= 2 * TQ:
        return length, length, 0, 1
    return TQ, NK, -HALF, length // TQ


def _padded_pitch(d):
    return d + d // 2


def _attn_kernel(q_ref, k_ref, v_ref, o_ref, qf, kf, vf, ob, lb, mb, qp, kp, vp, obp, lbp, mbp,
                 bias, bias_whole, pbuf, *, seq, pad):
    geometry = [_branch_geometry(seq, d) for d in DILATIONS]
    n_plain = sum(1 for g in geometry if g[3] > 1)
    assert all(g[3] > 1 for g in geometry[:n_plain]) and len(geometry) == n_plain + 1
    d_whole = DILATIONS[-1]
    pitch = _padded_pitch(d_whole)
    head_a = {tq: lax.broadcasted_iota(jnp.int32, (tq, LANES), 1) < HEAD_DIM
              for tq in {g[0] for g in geometry}}

    qf[...] = q_ref[0].astype(F32)
    zeros = jnp.zeros((pad, LANES), F32)
    for src, dst in ((k_ref, kf), (v_ref, vf)):
        dst[0:pad, :] = zeros
        dst[pad + seq:pad + seq + pad, :] = zeros
        dst[pad:pad + seq, :] = src[0].astype(F32)

    def pad_copy(g, carry):
        for src, dst in ((q_ref, qp), (k_ref, kp), (v_ref, vp)):
            dst[pl.ds(pl.multiple_of(g * pitch, 8), d_whole), :] = (
                src[0, pl.ds(pl.multiple_of(g * d_whole, d_whole), d_whole), :].astype(F32))
        return carry

    lax.fori_loop(0, seq // d_whole, pad_copy, 0, unroll=8)

    r = lax.broadcasted_iota(jnp.int32, (TQ, NK), 0)
    c = lax.broadcasted_iota(jnp.int32, (TQ, NK), 1)
    band = (c >= r) & (c <= r + 2 * HALF)
    bias[0] = jnp.where(band & (c >= HALF), 0.0, NEG_INF)
    bias[1] = jnp.where(band, 0.0, NEG_INF)
    bias[2] = jnp.where(band & (c < HALF + TQ), 0.0, NEG_INF)
    rw = lax.broadcasted_iota(jnp.int32, bias_whole.shape, 0)
    cw = lax.broadcasted_iota(jnp.int32, bias_whole.shape, 1)
    bias_whole[...] = jnp.where(jnp.abs(rw - cw) <= HALF, 0.0, NEG_INF)

    def rows(start, size, d):
        return pl.ds(start, size) if d == 1 else pl.ds(start, size, stride=d)

    n_groups = seq // GROUP_ROWS

    def group_blocks(br, t):
        tq, _, _, nblk = geometry[br]
        per_group = GROUP_ROWS // tq
        if nblk >= per_group:
            ngrp = nblk // per_group
            res, g = (t, 0) if ngrp == 1 else (t // ngrp, t % ngrp)
            static = isinstance(t, int) or ngrp == 1
            out = []
            for j in range(per_group):
                idx = 1
                if j == 0:
                    idx = (0 if g == 0 else 1) if static else jnp.where(g == 0, 0, 1)
                elif j == per_group - 1:
                    idx = (2 if g == ngrp - 1 else 1) if static else jnp.where(g == ngrp - 1, 2, 1)
                out.append((g * per_group + j, res, idx))
            return out
        per = per_group // nblk
        return [(i, t * per + jr, 0 if i == 0 else (2 if i == nblk - 1 else 1))
                for jr in range(per) for i in range(nblk)]

    def score_block(br, slot, n, blk):
        d = DILATIONS[br]
        tq, nk, koff, nblk = geometry[br]
        i, res, bias_idx = blk
        q0 = res + d * tq * i
        k0 = pad + res + d * (tq * i + koff)
        if nblk == 1:
            qb = qp[pl.ds(res, tq, stride=pitch), :]
            kb = kp[pl.ds(res, nk, stride=pitch), :].astype(BF16)
            b = bias_whole[...]
        else:
            qb = qf[rows(q0, tq, d), :]
            kb = kf[rows(k0, nk, d), :].astype(BF16)
            b = bias[bias_idx]
        q2 = jnp.concatenate([jnp.where(head_a[tq], qb, 0.0), jnp.where(head_a[tq], 0.0, qb)],
                             axis=0).astype(BF16)
        s2 = lax.dot_general(q2, kb, (((1,), (1,)), ((), ())), preferred_element_type=F32)
        ms = []
        for h in range(2):
            s = s2[h * tq:(h + 1) * tq] + b
            m = jnp.max(s, axis=1, keepdims=True)
            pbuf[slot, pl.ds((2 * n + h) * tq, tq), 0:nk] = jnp.exp2(s - m).astype(BF16)
            ms.append(m)
        m2 = jnp.where(head_a[tq], ms[0], ms[1])
        if nblk == 1:
            mbp[pl.ds(res, tq, stride=pitch), :] = m2
        else:
            mb[br, rows(q0, tq, d), :] = m2

    def value_block(br, slot, n, blk):
        d = DILATIONS[br]
        tq, nk, koff, nblk = geometry[br]
        i, res, _ = blk
        q0 = res + d * tq * i
        k0 = pad + res + d * (tq * i + koff)
        if nblk == 1:
            vb = vp[pl.ds(res, nk, stride=pitch), :].astype(BF16)
        else:
            vb = vf[rows(k0, nk, d), :].astype(BF16)
        r2 = jnp.dot(pbuf[slot, pl.ds(2 * n * tq, 2 * tq), 0:nk],
                     jnp.concatenate([vb, jnp.ones_like(vb)], axis=1),
                     preferred_element_type=F32)
        acc = jnp.where(head_a[tq], r2[:tq, :LANES], r2[tq:, :LANES])
        l = jnp.where(head_a[tq], r2[:tq, LANES:], r2[tq:, LANES:])
        if nblk == 1:
            obp[pl.ds(res, tq, stride=pitch), :] = acc
            lbp[pl.ds(res, tq, stride=pitch), :] = l
        else:
            ob[br, rows(q0, tq, d), :] = acc
            lb[br, rows(q0, tq, d), :] = l

    def stage(prev, cur):
        v = [] if prev is None else [
            functools.partial(value_block, prev[0], prev[2], n, blk)
            for n, blk in enumerate(group_blocks(prev[0], prev[1]))]
        s = [] if cur is None else [
            functools.partial(score_block, cur[0], cur[2], n, blk)
            for n, blk in enumerate(group_blocks(cur[0], cur[1]))]
        for j in range(0, max(len(v), len(s)), STAGE_CHUNK):
            for thunks in (v, s):
                for thunk in thunks[j:j + STAGE_CHUNK]:
                    thunk()

    assert n_groups % 2 == 0
    last = n_groups - 1
    stage(None, (0, 0, 0))
    for br in range(len(DILATIONS)):
        def body(t, carry, br=br):
            stage((br, t - 1, (t - 1) % 2), (br, t, t % 2))
            return carry

        lax.fori_loop(1, n_groups, body, 0)
        stage((br, last, last % 2), (br + 1, 0, 0) if br + 1 < len(DILATIONS) else None)

    chunk = 256

    padded_chunk = chunk // d_whole * pitch

    def combine(j, carry):
        sl = pl.ds(pl.multiple_of(j * chunk, chunk), chunk)
        slp = pl.ds(pl.multiple_of(j * padded_chunk, 8), padded_chunk)

        def token_rows(ref):
            x = ref[slp, :].reshape(chunk // d_whole, pitch, LANES)
            return x[:, :d_whole, :].reshape(chunk, LANES)

        ms = [mb[i, sl, :] for i in range(n_plain)] + [token_rows(mbp)]
        accs = [ob[i, sl, :] for i in range(n_plain)] + [token_rows(obp)]
        ls = [lb[i, sl, :] for i in range(n_plain)] + [token_rows(lbp)]
        mx = functools.reduce(jnp.maximum, ms)
        w = [jnp.exp2(m - mx) for m in ms]
        num = sum(wi * a for wi, a in zip(w, accs))
        den = sum(wi * l for wi, l in zip(w, ls))
        o_ref[0, sl, :] = (num / den).astype(o_ref.dtype)
        return carry

    lax.fori_loop(0, seq // chunk, combine, 0)


def _attention(q, k, v):
    b, s, _ = q.shape
    geometry = [_branch_geometry(s, d) for d in DILATIONS]
    pad = max(-koff * d for d, (_, _, koff, _) in zip(DILATIONS, geometry))
    whole = max(tq for tq, _, _, nblk in geometry if nblk == 1)
    nk_max = max(nk for _, nk, _, _ in geometry)
    n_plain = len(DILATIONS) - 1
    padded_rows = s // DILATIONS[-1] * _padded_pitch(DILATIONS[-1])
    spec = pl.BlockSpec((1, s, LANES), lambda i, j: (i, 0, j))
    return pl.pallas_call(
        functools.partial(_attn_kernel, seq=s, pad=pad),
        grid=(b, ATTN_WIDTH // LANES),
        in_specs=[spec, spec, spec],
        out_specs=spec,
        out_shape=jax.ShapeDtypeStruct((b, s, ATTN_WIDTH), BF16),
        scratch_shapes=[
            pltpu.VMEM((s, LANES), F32),
            pltpu.VMEM((s + 2 * pad, LANES), F32),
            pltpu.VMEM((s + 2 * pad, LANES), F32),
            pltpu.VMEM((n_plain, s, LANES), F32),
            pltpu.VMEM((n_plain, s, LANES), F32),
            pltpu.VMEM((n_plain, s, LANES), F32),
        ] + [pltpu.VMEM((padded_rows, LANES), F32)] * 6 + [
            pltpu.VMEM((3, TQ, NK), F32),
            pltpu.VMEM((whole, whole), F32),
            pltpu.VMEM((2, 2 * GROUP_ROWS, nk_max), BF16),
        ],
        compiler_params=pltpu.CompilerParams(
            dimension_semantics=("parallel", "parallel"), vmem_limit_bytes=56 * 2 ** 20),
        name="attn",
    )(q, k, v)


def _post_kernel(a_ref, u_ref, up_ref, un_ref, gb_ref, x_ref, cw_ref, ga_ref, gc_ref, wo_ref,
                 gp_ref, g1_ref, wgu_ref, wd_ref, g2_ref, o_ref, *, tiles_per_seq):
    t = pl.program_id(0) % tiles_per_seq
    u = u_ref[...].astype(F32)
    tm = u.shape[0]
    prev_row = jnp.where(t == 0, 0.0, up_ref[7:8, :].astype(F32))
    next_row = jnp.where(t == tiles_per_seq - 1, 0.0, un_ref[0:1, :].astype(F32))
    row = lax.broadcasted_iota(jnp.int32, u.shape, 0)
    u_prev = jnp.where(row == 0, prev_row, pltpu.roll(u, 1, 0))
    u_next = jnp.where(row == tm - 1, next_row, pltpu.roll(u, tm - 1, 0))
    y = cw_ref[0:1, :] * u_prev + cw_ref[1:2, :] * u + cw_ref[2:3, :] * u_next
    conv = gb_ref[...].astype(F32) * y
    slab = tm // POST_SLABS
    slabs = [slice(j * slab, (j + 1) * slab) for j in range(POST_SLABS)]

    def mix_slab(sl):
        na = _rms(a_ref[sl, :].astype(F32), ga_ref[...]).astype(BF16)
        nc = _rms(conv[sl], gc_ref[...]).astype(BF16)
        mix = (jnp.dot(na, wo_ref[0:ATTN_WIDTH, :], preferred_element_type=F32)
               + jnp.dot(nc, wo_ref[ATTN_WIDTH:, :], preferred_element_type=F32))
        return x_ref[sl, :] + _rms(mix, gp_ref[...])

    def ffn_chunk(state, c):
        if "h" not in state:
            state["h"] = _rms(state["x"], g1_ref[...]).astype(BF16)
        h = state["h"]
        start, size = FFN_CHUNKS[c]
        g = jnp.dot(h, wgu_ref[:, start:start + size], preferred_element_type=F32)
        up = jnp.dot(h, wgu_ref[:, FFN_HIDDEN + start:FFN_HIDDEN + start + size],
                     preferred_element_type=F32)
        act = (g * jax.nn.sigmoid(g) * up).astype(BF16)
        part = jnp.dot(act, wd_ref[start:start + size, :], preferred_element_type=F32)
        state["f"] = part if "f" not in state else state["f"] + part
        if c == len(FFN_CHUNKS) - 1:
            o_ref[state["sl"], :] = state["x"] + _rms(state["f"], g2_ref[...])

    states = {}
    for op, j in POST_ORDER:
        if op == "mix":
            states[j] = {"sl": slabs[j], "x": mix_slab(slabs[j])}
        else:
            ffn_chunk(states[j], op)


def _post(attn2, u, gb, x2, conv_w, g_attn, g_conv, w_out, g_post, g_pre, w_gu, w_down, g_post2,
          layer, seq):
    n = x2.shape[0]
    row = lambda i: (i, 0)
    halo = POST_TILE // 8
    vec = lambda width: pl.BlockSpec((None, 1, width), lambda i: (layer, 0, 0))
    resident = lambda r, c: pl.BlockSpec((r, c), lambda i: (0, 0), pipeline_mode=pl.Buffered(1))
    return pl.pallas_call(
        functools.partial(_post_kernel, tiles_per_seq=seq // POST_TILE),
        grid=(n // POST_TILE,),
        in_specs=[
            pl.BlockSpec((POST_TILE, ATTN_WIDTH), row),
            pl.BlockSpec((POST_TILE, CONV_WIDTH), row),
            pl.BlockSpec((8, CONV_WIDTH), lambda i: (jnp.maximum(i * halo - 1, 0), 0)),
            pl.BlockSpec((8, CONV_WIDTH), lambda i: (jnp.minimum((i + 1) * halo, n // 8 - 1), 0)),
            pl.BlockSpec((POST_TILE, CONV_WIDTH), row),
            pl.BlockSpec((POST_TILE, D_MODEL), row),
            pl.BlockSpec((None, CONV_K, CONV_WIDTH), lambda i: (layer, 0, 0)),
            vec(ATTN_WIDTH),
            vec(CONV_WIDTH),
            resident(D_MODEL, D_MODEL),
            vec(D_MODEL),
            vec(D_MODEL),
            resident(D_MODEL, 2 * FFN_HIDDEN),
            resident(FFN_HIDDEN, D_MODEL),
            vec(D_MODEL),
        ],
        out_specs=pl.BlockSpec((POST_TILE, D_MODEL), row),
        out_shape=jax.ShapeDtypeStruct((n, D_MODEL), F32),
        compiler_params=pltpu.CompilerParams(
            dimension_semantics=("parallel",), vmem_limit_bytes=58 * 2 ** 20),
        name="post",
    )(attn2, u, u, u, gb, x2, conv_w, g_attn, g_conv, w_out, g_post, g_pre, w_gu, w_down, g_post2)


def _rotary_tables(positions):
    half = ROPE_DIM // 2
    inv_freq = ROPE_THETA ** (-jnp.arange(0, ROPE_DIM, 2, dtype=F32) / ROPE_DIM)
    in_head = jnp.arange(LANES) % HEAD_DIM
    ang = positions.astype(F32).reshape(-1, 1) * inv_freq
    cos = jnp.tile(jnp.cos(ang), (1, LANES // half))
    sin = jnp.tile(jnp.sin(ang), (1, LANES // half))
    c = jnp.where(in_head < ROPE_DIM, cos, 1.0)
    s_up = jnp.where((in_head >= half) & (in_head < ROPE_DIM), sin, 0.0)
    s_dn = jnp.where(in_head < half, -sin, 0.0)
    return c, s_up, s_dn


def kernel(x, positions, pre_mix_norm, w_in, conv_w, attn_out_norm, conv_out_norm, w_out,
           post_mix_norm, pre_ffn_norm, w_gate_up, w_down, post_ffn_norm):
    b, s, d = x.shape
    depth = w_in.shape[0]
    n = b * s
    cos, sin_up, sin_dn = _rotary_tables(positions)
    vec = lambda g: g.reshape(depth, 1, g.shape[-1])
    w_in_b = w_in[0].astype(BF16)
    x2 = x.reshape(n, d)
    for l in range(depth):
        casts = [(w_out, l), (w_gate_up, l), (w_down, l)] + ([(w_in, l + 1)] if l + 1 < depth else [])
        (q, k, v, u, gb), cast = _in_proj(x2, vec(pre_mix_norm), w_in_b, cos, sin_up, sin_dn, l,
                                          casts)
        w_out_b, w_gu_b, w_down_b = cast[:3]
        w_in_b = cast[3] if l + 1 < depth else None
        attn = _attention(q.reshape(b, s, ATTN_WIDTH), k.reshape(b, s, ATTN_WIDTH),
                          v.reshape(b, s, ATTN_WIDTH))
        x2 = _post(attn.reshape(n, ATTN_WIDTH), u, gb, x2, conv_w, vec(attn_out_norm),
                   vec(conv_out_norm), w_out_b, vec(post_mix_norm), vec(pre_ffn_norm), w_gu_b,
                   w_down_b, vec(post_ffn_norm), l, s)
    return x2.reshape(b, s, d)
```

```python
import functools

import jax
import jax.numpy as jnp
from jax import lax
from jax.experimental import pallas as pl
from jax.experimental.pallas import tpu as pltpu

D_MODEL = 1024
HEAD_DIM = 64
ATTN_WIDTH = 512
CONV_WIDTH = 512
N_ATTN_HEADS = 8
IN_PROJ_WIDTH = 3 * ATTN_WIDTH + 3 * CONV_WIDTH
CONV_K = 3
ROPE_DIM = 16
ROPE_THETA = 500000.0
DILATIONS = (1, 4, 16)
HALF = 64
FFN_HIDDEN = 2816
RMS_EPS = 1e-6
NEG_INF = -1e30
LOG2_E = 1.4426950408889634

LANES = 128
ROW_TILE = 1024
TQ = 128
NK = TQ + 2 * HALF
GROUP_ROWS = 8 * TQ
VALUES_FIRST_NUM, VALUES_FIRST_DEN = 1, 2
FFN_CHUNKS = ((0, 1536), (1536, 1280))
POST_TILE = 1024
POST_SLABS = 4
POST_ORDER = (tuple(("mix", j) for j in range(POST_SLABS))
              + tuple((c, j) for j in range(POST_SLABS) for c in range(len(FFN_CHUNKS))))

F32 = jnp.float32
BF16 = jnp.bfloat16


def _rms(x, g):
    return x * lax.rsqrt(jnp.mean(x * x, axis=-1, keepdims=True) + RMS_EPS) * g


def _in_proj_kernel(x_ref, g_ref, w_ref, c_ref, s1_ref, s2_ref, *refs):
    n_cast = (len(refs) - 5) // 2
    cast_src = refs[:n_cast]
    q_ref, k_ref, v_ref, u_ref, gb_ref = refs[n_cast:n_cast + 5]
    cast_dst = refs[n_cast + 5:]
    h = _rms(x_ref[...], g_ref[...]).astype(BF16)

    def proj(c):
        return jnp.dot(h, w_ref[:, c * 512:(c + 1) * 512], preferred_element_type=F32)

    cos, sin_up, sin_dn = c_ref[...], s1_ref[...], s2_ref[...]

    def rotary(t):
        parts = []
        for j in range(ATTN_WIDTH // LANES):
            tj = t[:, j * LANES:(j + 1) * LANES]
            parts.append(tj * cos + pltpu.roll(tj, 8, 1) * sin_up
                         + pltpu.roll(tj, LANES - 8, 1) * sin_dn)
        return jnp.concatenate(parts, axis=1)

    q_ref[...] = (rotary(proj(0)) * (HEAD_DIM ** -0.5 * LOG2_E)).astype(BF16)
    k_ref[...] = rotary(proj(1)).astype(BF16)
    v_ref[...] = proj(2).astype(BF16)
    u_ref[...] = (proj(5) * proj(3)).astype(BF16)
    gb_ref[...] = proj(4).astype(BF16)
    for src, dst in zip(cast_src, cast_dst):
        dst[...] = src[...].astype(BF16)


def _in_proj(x2, g, w, cos, sin_up, sin_dn, layer, casts):
    n = x2.shape[0]
    steps = n // ROW_TILE
    row = lambda i: (i, 0)
    out = jax.ShapeDtypeStruct((n, 512), BF16)
    cast_in_specs, cast_out_specs, cast_shapes = [], [], []
    for wt, lyr in casts:
        r, c = wt.shape[1:]
        cast_in_specs.append(pl.BlockSpec((None, r // steps, c), lambda i, lyr=lyr: (lyr, i, 0)))
        cast_out_specs.append(pl.BlockSpec((r // steps, c), row))
        cast_shapes.append(jax.ShapeDtypeStruct((r, c), BF16))
    res = pl.pallas_call(
        _in_proj_kernel,
        grid=(steps,),
        in_specs=[
            pl.BlockSpec((ROW_TILE, D_MODEL), row),
            pl.BlockSpec((None, 1, D_MODEL), lambda i: (layer, 0, 0)),
            pl.BlockSpec((D_MODEL, IN_PROJ_WIDTH), lambda i: (0, 0), pipeline_mode=pl.Buffered(1)),
            pl.BlockSpec((ROW_TILE, LANES), row),
            pl.BlockSpec((ROW_TILE, LANES), row),
            pl.BlockSpec((ROW_TILE, LANES), row),
        ] + cast_in_specs,
        out_specs=[pl.BlockSpec((ROW_TILE, 512), row)] * 5 + cast_out_specs,
        out_shape=[out] * 5 + cast_shapes,
        compiler_params=pltpu.CompilerParams(
            dimension_semantics=("parallel",), vmem_limit_bytes=48 * 2 ** 20),
        name="in_proj",
    )(x2, g, w, cos, sin_up, sin_dn, *[wt for wt, _ in casts])
    return res[:5], res[5:]


def _branch_geometry(seq, d):
    length = seq // d
    if length <= 2 * TQ:
        return length, length, 0, 1
    return TQ, NK, -HALF, length // TQ


def _padded_pitch(d):
    return d + d // 2


def _attn_kernel(q_ref, k_ref, v_ref, o_ref, qf, kf, vf, ob, lb, mb, qp, kp, vp, obp, lbp, mbp,
                 bias, bias_whole, pbuf_even, pbuf_odd, *, seq, pad):
    pbuf = (pbuf_even, pbuf_odd)
    geometry = [_branch_geometry(seq, d) for d in DILATIONS]
    n_plain = sum(1 for g in geometry if g[3] > 1)
    assert all(g[3] > 1 for g in geometry[:n_plain]) and len(geometry) == n_plain + 1
    d_whole = DILATIONS[-1]
    pitch = _padded_pitch(d_whole)
    head_a = {tq: lax.broadcasted_iota(jnp.int32, (tq, LANES), 1) < HEAD_DIM
              for tq in {g[0] for g in geometry}}

    qf[...] = q_ref[0].astype(F32)
    zeros = jnp.zeros((pad, LANES), F32)
    for src, dst in ((k_ref, kf), (v_ref, vf)):
        dst[0:pad, :] = zeros
        dst[pad + seq:pad + seq + pad, :] = zeros
        dst[pad:pad + seq, :] = src[0].astype(F32)

    def pad_copy(g, carry):
        for src, dst in ((q_ref, qp), (k_ref, kp), (v_ref, vp)):
            dst[pl.ds(pl.multiple_of(g * pitch, 8), d_whole), :] = (
                src[0, pl.ds(pl.multiple_of(g * d_whole, d_whole), d_whole), :].astype(F32))
        return carry

    lax.fori_loop(0, seq // d_whole, pad_copy, 0, unroll=8)

    r = lax.broadcasted_iota(jnp.int32, (TQ, NK), 0)
    c = lax.broadcasted_iota(jnp.int32, (TQ, NK), 1)
    band = (c >= r) & (c <= r + 2 * HALF)
    bias[0] = jnp.where(band & (c >= HALF), 0.0, NEG_INF)
    bias[1] = jnp.where(band, 0.0, NEG_INF)
    bias[2] = jnp.where(band & (c < HALF + TQ), 0.0, NEG_INF)
    rw = lax.broadcasted_iota(jnp.int32, bias_whole.shape, 0)
    cw = lax.broadcasted_iota(jnp.int32, bias_whole.shape, 1)
    bias_whole[...] = jnp.where(jnp.abs(rw - cw) <= HALF, 0.0, NEG_INF)

    def rows(start, size, d):
        return pl.ds(start, size) if d == 1 else pl.ds(start, size, stride=d)

    n_groups = seq // GROUP_ROWS

    def group_blocks(br, t):
        tq, _, _, nblk = geometry[br]
        per_group = GROUP_ROWS // tq
        if nblk >= per_group:
            ngrp = nblk // per_group
            res, g = (t, 0) if ngrp == 1 else (t // ngrp, t % ngrp)
            static = isinstance(t, int) or ngrp == 1
            out = []
            for j in range(per_group):
                idx = 1
                if j == 0:
                    idx = (0 if g == 0 else 1) if static else jnp.where(g == 0, 0, 1)
                elif j == per_group - 1:
                    idx = (2 if g == ngrp - 1 else 1) if static else jnp.where(g == ngrp - 1, 2, 1)
                out.append((g * per_group + j, res, idx))
            return out
        per = per_group // nblk
        return [(i, t * per + jr, 0 if i == 0 else (2 if i == nblk - 1 else 1))
                for jr in range(per) for i in range(nblk)]

    def score_block(br, slot, n, blk):
        d = DILATIONS[br]
        tq, nk, koff, nblk = geometry[br]
        i, res, bias_idx = blk
        q0 = res + d * tq * i
        k0 = pad + res + d * (tq * i + koff)
        if nblk == 1:
            qb = qp[pl.ds(res, tq, stride=pitch), :]
            kb = kp[pl.ds(res, nk, stride=pitch), :].astype(BF16)
            b = bias_whole[...]
        else:
            qb = qf[rows(q0, tq, d), :]
            kb = kf[rows(k0, nk, d), :].astype(BF16)
            b = bias[bias_idx]
        q2 = jnp.concatenate([jnp.where(head_a[tq], qb, 0.0), jnp.where(head_a[tq], 0.0, qb)],
                             axis=0).astype(BF16)
        s2 = lax.dot_general(q2, kb, (((1,), (1,)), ((), ())), preferred_element_type=F32)
        ms = []
        for h in range(2):
            s = s2[h * tq:(h + 1) * tq] + b
            m = jnp.max(s, axis=1, keepdims=True)
            pbuf[slot][pl.ds((2 * n + h) * tq, tq), 0:nk] = jnp.exp2(s - m).astype(BF16)
            ms.append(m)
        m2 = jnp.where(head_a[tq], ms[0], ms[1])
        if nblk == 1:
            mbp[pl.ds(res, tq, stride=pitch), :] = m2
        else:
            mb[br, rows(q0, tq, d), :] = m2

    def value_block(br, slot, n, blk):
        d = DILATIONS[br]
        tq, nk, koff, nblk = geometry[br]
        i, res, _ = blk
        q0 = res + d * tq * i
        k0 = pad + res + d * (tq * i + koff)
        if nblk == 1:
            vb = vp[pl.ds(res, nk, stride=pitch), :].astype(BF16)
        else:
            vb = vf[rows(k0, nk, d), :].astype(BF16)
        r2 = jnp.dot(pbuf[slot][pl.ds(2 * n * tq, 2 * tq), 0:nk],
                     jnp.concatenate([vb, jnp.ones_like(vb)], axis=1),
                     preferred_element_type=F32)
        acc = jnp.where(head_a[tq], r2[:tq, :LANES], r2[tq:, :LANES])
        l = jnp.where(head_a[tq], r2[:tq, LANES:], r2[tq:, LANES:])
        if nblk == 1:
            obp[pl.ds(res, tq, stride=pitch), :] = acc
            lbp[pl.ds(res, tq, stride=pitch), :] = l
        else:
            ob[br, rows(q0, tq, d), :] = acc
            lb[br, rows(q0, tq, d), :] = l

    def stage(prev, cur):
        v = [] if prev is None else [
            functools.partial(value_block, prev[0], prev[2], n, blk)
            for n, blk in enumerate(group_blocks(prev[0], prev[1]))]
        s = [] if cur is None else [
            functools.partial(score_block, cur[0], cur[2], n, blk)
            for n, blk in enumerate(group_blocks(cur[0], cur[1]))]
        head = len(v) * VALUES_FIRST_NUM // VALUES_FIRST_DEN
        for thunk in v[:head] + s + v[head:]:
            thunk()

    assert n_groups % 2 == 0
    last = n_groups - 1
    stage(None, (0, 0, 0))
    for br in range(len(DILATIONS)):
        for t in range(1, n_groups):
            stage((br, t - 1, (t - 1) % 2), (br, t, t % 2))
        stage((br, last, last % 2), (br + 1, 0, 0) if br + 1 < len(DILATIONS) else None)

    chunk = 256

    padded_chunk = chunk // d_whole * pitch

    def combine(j, carry):
        sl = pl.ds(pl.multiple_of(j * chunk, chunk), chunk)
        slp = pl.ds(pl.multiple_of(j * padded_chunk, 8), padded_chunk)

        def token_rows(ref):
            x = ref[slp, :].reshape(chunk // d_whole, pitch, LANES)
            return x[:, :d_whole, :].reshape(chunk, LANES)

        ms = [mb[i, sl, :] for i in range(n_plain)] + [token_rows(mbp)]
        accs = [ob[i, sl, :] for i in range(n_plain)] + [token_rows(obp)]
        ls = [lb[i, sl, :] for i in range(n_plain)] + [token_rows(lbp)]
        mx = functools.reduce(jnp.maximum, ms)
        w = [jnp.exp2(m - mx) for m in ms]
        num = sum(wi * a for wi, a in zip(w, accs))
        den = sum(wi * l for wi, l in zip(w, ls))
        o_ref[0, sl, :] = (num / den).astype(o_ref.dtype)
        return carry

    lax.fori_loop(0, seq // chunk, combine, 0)


def _attention(q, k, v):
    b, s, _ = q.shape
    geometry = [_branch_geometry(s, d) for d in DILATIONS]
    pad = max(-koff * d for d, (_, _, koff, _) in zip(DILATIONS, geometry))
    whole = max(tq for tq, _, _, nblk in geometry if nblk == 1)
    nk_max = max(nk for _, nk, _, _ in geometry)
    n_plain = len(DILATIONS) - 1
    padded_rows = s // DILATIONS[-1] * _padded_pitch(DILATIONS[-1])
    spec = pl.BlockSpec((1, s, LANES), lambda i, j: (i, 0, j))
    return pl.pallas_call(
        functools.partial(_attn_kernel, seq=s, pad=pad),
        grid=(b, ATTN_WIDTH // LANES),
        in_specs=[spec, spec, spec],
        out_specs=spec,
        out_shape=jax.ShapeDtypeStruct((b, s, ATTN_WIDTH), BF16),
        scratch_shapes=[
            pltpu.VMEM((s, LANES), F32),
            pltpu.VMEM((s + 2 * pad, LANES), F32),
            pltpu.VMEM((s + 2 * pad, LANES), F32),
            pltpu.VMEM((n_plain, s, LANES), F32),
            pltpu.VMEM((n_plain, s, LANES), F32),
            pltpu.VMEM((n_plain, s, LANES), F32),
        ] + [pltpu.VMEM((padded_rows, LANES), F32)] * 6 + [
            pltpu.VMEM((3, TQ, NK), F32),
            pltpu.VMEM((whole, whole), F32),
            pltpu.VMEM((2 * GROUP_ROWS, nk_max), BF16),
            pltpu.VMEM((2 * GROUP_ROWS, nk_max), BF16),
        ],
        compiler_params=pltpu.CompilerParams(
            dimension_semantics=("parallel", "parallel"), vmem_limit_bytes=56 * 2 ** 20),
        name="attn",
    )(q, k, v)


def _post_kernel(a_ref, u_ref, up_ref, un_ref, gb_ref, x_ref, cw_ref, ga_ref, gc_ref, wo_ref,
                 gp_ref, g1_ref, wgu_ref, wd_ref, g2_ref, o_ref, *, tiles_per_seq):
    t = pl.program_id(0) % tiles_per_seq
    u = u_ref[...].astype(F32)
    tm = u.shape[0]
    prev_row = jnp.where(t == 0, 0.0, up_ref[7:8, :].astype(F32))
    next_row = jnp.where(t == tiles_per_seq - 1, 0.0, un_ref[0:1, :].astype(F32))
    row = lax.broadcasted_iota(jnp.int32, u.shape, 0)
    u_prev = jnp.where(row == 0, prev_row, pltpu.roll(u, 1, 0))
    u_next = jnp.where(row == tm - 1, next_row, pltpu.roll(u, tm - 1, 0))
    y = cw_ref[0:1, :] * u_prev + cw_ref[1:2, :] * u + cw_ref[2:3, :] * u_next
    conv = gb_ref[...].astype(F32) * y
    slab = tm // POST_SLABS
    slabs = [slice(j * slab, (j + 1) * slab) for j in range(POST_SLABS)]

    def mix_slab(sl):
        na = _rms(a_ref[sl, :].astype(F32), ga_ref[...]).astype(BF16)
        nc = _rms(conv[sl], gc_ref[...]).astype(BF16)
        mix = (jnp.dot(na, wo_ref[0:ATTN_WIDTH, :], preferred_element_type=F32)
               + jnp.dot(nc, wo_ref[ATTN_WIDTH:, :], preferred_element_type=F32))
        return x_ref[sl, :] + _rms(mix, gp_ref[...])

    def ffn_chunk(state, c):
        if "h" not in state:
            state["h"] = _rms(state["x"], g1_ref[...]).astype(BF16)
        h = state["h"]
        start, size = FFN_CHUNKS[c]
        g = jnp.dot(h, wgu_ref[:, start:start + size], preferred_element_type=F32)
        up = jnp.dot(h, wgu_ref[:, FFN_HIDDEN + start:FFN_HIDDEN + start + size],
                     preferred_element_type=F32)
        act = (g * jax.nn.sigmoid(g) * up).astype(BF16)
        part = jnp.dot(act, wd_ref[start:start + size, :], preferred_element_type=F32)
        state["f"] = part if "f" not in state else state["f"] + part
        if c == len(FFN_CHUNKS) - 1:
            o_ref[state["sl"], :] = state["x"] + _rms(state["f"], g2_ref[...])

    states = {}
    for op, j in POST_ORDER:
        if op == "mix":
            states[j] = {"sl": slabs[j], "x": mix_slab(slabs[j])}
        else:
            ffn_chunk(states[j], op)


def _post(attn2, u, gb, x2, conv_w, g_attn, g_conv, w_out, g_post, g_pre, w_gu, w_down, g_post2,
          layer, seq):
    n = x2.shape[0]
    row = lambda i: (i, 0)
    halo = POST_TILE // 8
    vec = lambda width: pl.BlockSpec((None, 1, width), lambda i: (layer, 0, 0))
    resident = lambda r, c: pl.BlockSpec((r, c), lambda i: (0, 0), pipeline_mode=pl.Buffered(1))
    return pl.pallas_call(
        functools.partial(_post_kernel, tiles_per_seq=seq // POST_TILE),
        grid=(n // POST_TILE,),
        in_specs=[
            pl.BlockSpec((POST_TILE, ATTN_WIDTH), row),
            pl.BlockSpec((POST_TILE, CONV_WIDTH), row),
            pl.BlockSpec((8, CONV_WIDTH), lambda i: (jnp.maximum(i * halo - 1, 0), 0)),
            pl.BlockSpec((8, CONV_WIDTH), lambda i: (jnp.minimum((i + 1) * halo, n // 8 - 1), 0)),
            pl.BlockSpec((POST_TILE, CONV_WIDTH), row),
            pl.BlockSpec((POST_TILE, D_MODEL), row),
            pl.BlockSpec((None, CONV_K, CONV_WIDTH), lambda i: (layer, 0, 0)),
            vec(ATTN_WIDTH),
            vec(CONV_WIDTH),
            resident(D_MODEL, D_MODEL),
            vec(D_MODEL),
            vec(D_MODEL),
            resident(D_MODEL, 2 * FFN_HIDDEN),
            resident(FFN_HIDDEN, D_MODEL),
            vec(D_MODEL),
        ],
        out_specs=pl.BlockSpec((POST_TILE, D_MODEL), row),
        out_shape=jax.ShapeDtypeStruct((n, D_MODEL), F32),
        compiler_params=pltpu.CompilerParams(
            dimension_semantics=("parallel",), vmem_limit_bytes=58 * 2 ** 20),
        name="post",
    )(attn2, u, u, u, gb, x2, conv_w, g_attn, g_conv, w_out, g_post, g_pre, w_gu, w_down, g_post2)


def _rotary_tables(positions):
    half = ROPE_DIM // 2
    inv_freq = ROPE_THETA ** (-jnp.arange(0, ROPE_DIM, 2, dtype=F32) / ROPE_DIM)
    in_head = jnp.arange(LANES) % HEAD_DIM
    ang = positions.astype(F32).reshape(-1, 1) * inv_freq
    cos = jnp.tile(jnp.cos(ang), (1, LANES // half))
    sin = jnp.tile(jnp.sin(ang), (1, LANES // half))
    c = jnp.where(in_head < ROPE_DIM, cos, 1.0)
    s_up = jnp.where((in_head >= half) & (in_head < ROPE_DIM), sin, 0.0)
    s_dn = jnp.where(in_head < half, -sin, 0.0)
    return c, s_up, s_dn


def kernel(x, positions, pre_mix_norm, w_in, conv_w, attn_out_norm, conv_out_norm, w_out,
           post_mix_norm, pre_ffn_norm, w_gate_up, w_down, post_ffn_norm):
    b, s, d = x.shape
    depth = w_in.shape[0]
    n = b * s
    cos, sin_up, sin_dn = _rotary_tables(positions)
    vec = lambda g: g.reshape(depth, 1, g.shape[-1])
    w_in_b = w_in[0].astype(BF16)
    x2 = x.reshape(n, d)
    for l in range(depth):
        casts = [(w_out, l), (w_gate_up, l), (w_down, l)] + ([(w_in, l + 1)] if l + 1 < depth else [])
        (q, k, v, u, gb), cast = _in_proj(x2, vec(pre_mix_norm), w_in_b, cos, sin_up, sin_dn, l,
                                          casts)
        w_out_b, w_gu_b, w_down_b = cast[:3]
        w_in_b = cast[3] if l + 1 < depth else None
        attn = _attention(q.reshape(b, s, ATTN_WIDTH), k.reshape(b, s, ATTN_WIDTH),
                          v.reshape(b, s, ATTN_WIDTH))
        x2 = _post(attn.reshape(n, ATTN_WIDTH), u, gb, x2, conv_w, vec(attn_out_norm),
                   vec(conv_out_norm), w_out_b, vec(post_mix_norm), vec(pre_ffn_norm), w_gu_b,
                   w_down_b, vec(post_ffn_norm), l, s)
    return x2.reshape(b, s, d)
```

```python
import functools

import jax
import jax.numpy as jnp
from jax import lax
from jax.experimental import pallas as pl
from jax.experimental.pallas import tpu as pltpu

D_MODEL = 1024
HEAD_DIM = 64
ATTN_WIDTH = 512
CONV_WIDTH = 512
N_ATTN_HEADS = 8
IN_PROJ_WIDTH = 3 * ATTN_WIDTH + 3 * CONV_WIDTH
CONV_K = 3
ROPE_DIM = 16
ROPE_THETA = 500000.0
DILATIONS = (1, 4, 16)
HALF = 64
FFN_HIDDEN = 2816
RMS_EPS = 1e-6
NEG_INF = -1e30
LOG2_E = 1.4426950408889634

LANES = 128
ROW_TILE = 1024
TQ = 128
NK = TQ + 2 * HALF
GROUP_ROWS = 8 * TQ
VALUES_FIRST_NUM, VALUES_FIRST_DEN = 0, 2
FFN_CHUNKS = ((0, 1536), (1536, 1280))
POST_TILE = 1024
POST_SLABS = 4
POST_ORDER = (tuple(("mix", j) for j in range(POST_SLABS))
              + tuple((c, j) for j in range(POST_SLABS) for c in range(len(FFN_CHUNKS))))

F32 = jnp.float32
BF16 = jnp.bfloat16


def _rms(x, g):
    return x * lax.rsqrt(jnp.mean(x * x, axis=-1, keepdims=True) + RMS_EPS) * g


def _in_proj_kernel(x_ref, g_ref, w_ref, c_ref, s1_ref, s2_ref, *refs):
    n_cast = (len(refs) - 5) // 2
    cast_src = refs[:n_cast]
    q_ref, k_ref, v_ref, u_ref, gb_ref = refs[n_cast:n_cast + 5]
    cast_dst = refs[n_cast + 5:]
    h = _rms(x_ref[...], g_ref[...]).astype(BF16)

    def proj(c):
        return jnp.dot(h, w_ref[:, c * 512:(c + 1) * 512], preferred_element_type=F32)

    cos, sin_up, sin_dn = c_ref[...], s1_ref[...], s2_ref[...]

    def rotary(t):
        parts = []
        for j in range(ATTN_WIDTH // LANES):
            tj = t[:, j * LANES:(j + 1) * LANES]
            parts.append(tj * cos + pltpu.roll(tj, 8, 1) * sin_up
                         + pltpu.roll(tj, LANES - 8, 1) * sin_dn)
        return jnp.concatenate(parts, axis=1)

    q_ref[...] = (rotary(proj(0)) * (HEAD_DIM ** -0.5 * LOG2_E)).astype(BF16)
    k_ref[...] = rotary(proj(1)).astype(BF16)
    v_ref[...] = proj(2).astype(BF16)
    u_ref[...] = (proj(5) * proj(3)).astype(BF16)
    gb_ref[...] = proj(4).astype(BF16)
    for src, dst in zip(cast_src, cast_dst):
        dst[...] = src[...].astype(BF16)


def _in_proj(x2, g, w, cos, sin_up, sin_dn, layer, casts):
    n = x2.shape[0]
    steps = n // ROW_TILE
    row = lambda i: (i, 0)
    out = jax.ShapeDtypeStruct((n, 512), BF16)
    cast_in_specs, cast_out_specs, cast_shapes = [], [], []
    for wt, lyr in casts:
        r, c = wt.shape[1:]
        cast_in_specs.append(pl.BlockSpec((None, r // steps, c), lambda i, lyr=lyr: (lyr, i, 0)))
        cast_out_specs.append(pl.BlockSpec((r // steps, c), row))
        cast_shapes.append(jax.ShapeDtypeStruct((r, c), BF16))
    res = pl.pallas_call(
        _in_proj_kernel,
        grid=(steps,),
        in_specs=[
            pl.BlockSpec((ROW_TILE, D_MODEL), row),
            pl.BlockSpec((None, 1, D_MODEL), lambda i: (layer, 0, 0)),
            pl.BlockSpec((D_MODEL, IN_PROJ_WIDTH), lambda i: (0, 0), pipeline_mode=pl.Buffered(1)),
            pl.BlockSpec((ROW_TILE, LANES), row),
            pl.BlockSpec((ROW_TILE, LANES), row),
            pl.BlockSpec((ROW_TILE, LANES), row),
        ] + cast_in_specs,
        out_specs=[pl.BlockSpec((ROW_TILE, 512), row)] * 5 + cast_out_specs,
        out_shape=[out] * 5 + cast_shapes,
        compiler_params=pltpu.CompilerParams(
            dimension_semantics=("parallel",), vmem_limit_bytes=48 * 2 ** 20),
        name="in_proj",
    )(x2, g, w, cos, sin_up, sin_dn, *[wt for wt, _ in casts])
    return res[:5], res[5:]


def _branch_geometry(seq, d):
    length = seq // d
    if length <= 2 * TQ:
        return length, length, 0, 1
    return TQ, NK, -HALF, length // TQ


def _padded_pitch(d):
    return d + d // 2


def _attn_kernel(q_ref, k_ref, v_ref, o_ref, qf, kf, vf, ob, lb, mb, qp, kp, vp, obp, lbp, mbp,
                 bias, bias_whole, pbuf_even, pbuf_odd, *, seq, pad):
    pbuf = (pbuf_even, pbuf_odd)
    geometry = [_branch_geometry(seq, d) for d in DILATIONS]
    n_plain = sum(1 for g in geometry if g[3] > 1)
    assert all(g[3] > 1 for g in geometry[:n_plain]) and len(geometry) == n_plain + 1
    d_whole = DILATIONS[-1]
    pitch = _padded_pitch(d_whole)
    head_a = {tq: lax.broadcasted_iota(jnp.int32, (tq, LANES), 1) < HEAD_DIM
              for tq in {g[0] for g in geometry}}

    qf[...] = q_ref[0].astype(F32)
    zeros = jnp.zeros((pad, LANES), F32)
    for src, dst in ((k_ref, kf), (v_ref, vf)):
        dst[0:pad, :] = zeros
        dst[pad + seq:pad + seq + pad, :] = zeros
        dst[pad:pad + seq, :] = src[0].astype(F32)

    def pad_copies():
        for src, dst in ((q_ref, qp), (k_ref, kp), (v_ref, vp)):
            for g in range(seq // d_whole):
                dst[g * pitch:g * pitch + d_whole, :] = (
                    src[0, g * d_whole:(g + 1) * d_whole, :].astype(F32))

    r = lax.broadcasted_iota(jnp.int32, (TQ, NK), 0)
    c = lax.broadcasted_iota(jnp.int32, (TQ, NK), 1)
    band = (c >= r) & (c <= r + 2 * HALF)
    bias[0] = jnp.where(band & (c >= HALF), 0.0, NEG_INF)
    bias[1] = jnp.where(band, 0.0, NEG_INF)
    bias[2] = jnp.where(band & (c < HALF + TQ), 0.0, NEG_INF)
    rw = lax.broadcasted_iota(jnp.int32, bias_whole.shape, 0)
    cw = lax.broadcasted_iota(jnp.int32, bias_whole.shape, 1)
    bias_whole[...] = jnp.where(jnp.abs(rw - cw) <= HALF, 0.0, NEG_INF)

    def rows(start, size, d):
        return pl.ds(start, size) if d == 1 else pl.ds(start, size, stride=d)

    n_groups = seq // GROUP_ROWS

    def group_blocks(br, t):
        tq, _, _, nblk = geometry[br]
        per_group = GROUP_ROWS // tq
        if nblk >= per_group:
            ngrp = nblk // per_group
            res, g = (t, 0) if ngrp == 1 else (t // ngrp, t % ngrp)
            static = isinstance(t, int) or ngrp == 1
            out = []
            for j in range(per_group):
                idx = 1
                if j == 0:
                    idx = (0 if g == 0 else 1) if static else jnp.where(g == 0, 0, 1)
                elif j == per_group - 1:
                    idx = (2 if g == ngrp - 1 else 1) if static else jnp.where(g == ngrp - 1, 2, 1)
                out.append((g * per_group + j, res, idx))
            return out
        per = per_group // nblk
        return [(i, t * per + jr, 0 if i == 0 else (2 if i == nblk - 1 else 1))
                for jr in range(per) for i in range(nblk)]

    def score_block(br, slot, n, blk):
        d = DILATIONS[br]
        tq, nk, koff, nblk = geometry[br]
        i, res, bias_idx = blk
        q0 = res + d * tq * i
        k0 = pad + res + d * (tq * i + koff)
        if nblk == 1:
            qb = qp[pl.ds(res, tq, stride=pitch), :]
            kb = kp[pl.ds(res, nk, stride=pitch), :].astype(BF16)
            b = bias_whole[...]
        else:
            qb = qf[rows(q0, tq, d), :]
            kb = kf[rows(k0, nk, d), :].astype(BF16)
            b = bias[bias_idx]
        q2 = jnp.concatenate([jnp.where(head_a[tq], qb, 0.0), jnp.where(head_a[tq], 0.0, qb)],
                             axis=0).astype(BF16)
        s2 = lax.dot_general(q2, kb, (((1,), (1,)), ((), ())), preferred_element_type=F32)
        ms = []
        for h in range(2):
            s = s2[h * tq:(h + 1) * tq] + b
            m = jnp.max(s, axis=1, keepdims=True)
            pbuf[slot][pl.ds((2 * n + h) * tq, tq), 0:nk] = jnp.exp2(s - m).astype(BF16)
            ms.append(m)
        m2 = jnp.where(head_a[tq], ms[0], ms[1])
        if nblk == 1:
            mbp[pl.ds(res, tq, stride=pitch), :] = m2
        else:
            mb[br, rows(q0, tq, d), :] = m2

    def value_block(br, slot, n, blk):
        d = DILATIONS[br]
        tq, nk, koff, nblk = geometry[br]
        i, res, _ = blk
        q0 = res + d * tq * i
        k0 = pad + res + d * (tq * i + koff)
        if nblk == 1:
            vb = vp[pl.ds(res, nk, stride=pitch), :].astype(BF16)
        else:
            vb = vf[rows(k0, nk, d), :].astype(BF16)
        r2 = jnp.dot(pbuf[slot][pl.ds(2 * n * tq, 2 * tq), 0:nk],
                     jnp.concatenate([vb, jnp.ones_like(vb)], axis=1),
                     preferred_element_type=F32)
        acc = jnp.where(head_a[tq], r2[:tq, :LANES], r2[tq:, :LANES])
        l = jnp.where(head_a[tq], r2[:tq, LANES:], r2[tq:, LANES:])
        if nblk == 1:
            obp[pl.ds(res, tq, stride=pitch), :] = acc
            lbp[pl.ds(res, tq, stride=pitch), :] = l
        else:
            ob[br, rows(q0, tq, d), :] = acc
            lb[br, rows(q0, tq, d), :] = l

    def stage(prev, cur):
        v = [] if prev is None else [
            functools.partial(value_block, prev[0], prev[2], n, blk)
            for n, blk in enumerate(group_blocks(prev[0], prev[1]))]
        s = [] if cur is None else [
            functools.partial(score_block, cur[0], cur[2], n, blk)
            for n, blk in enumerate(group_blocks(cur[0], cur[1]))]
        head = len(v) * VALUES_FIRST_NUM // VALUES_FIRST_DEN
        for thunk in v[:head] + s + v[head:]:
            thunk()

    assert n_groups % 2 == 0
    last = n_groups - 1
    stage(None, (0, 0, 0))
    for br in range(len(DILATIONS)):
        for t in range(1, n_groups):
            stage((br, t - 1, (t - 1) % 2), (br, t, t % 2))
        stage((br, last, last % 2), (br + 1, 0, 0) if br + 1 < len(DILATIONS) else None)
        if br == 0:
            pad_copies()

    chunk = 256

    padded_chunk = chunk // d_whole * pitch

    def combine(j, carry):
        sl = pl.ds(pl.multiple_of(j * chunk, chunk), chunk)
        slp = pl.ds(pl.multiple_of(j * padded_chunk, 8), padded_chunk)

        def token_rows(ref):
            x = ref[slp, :].reshape(chunk // d_whole, pitch, LANES)
            return x[:, :d_whole, :].reshape(chunk, LANES)

        ms = [mb[i, sl, :] for i in range(n_plain)] + [token_rows(mbp)]
        accs = [ob[i, sl, :] for i in range(n_plain)] + [token_rows(obp)]
        ls = [lb[i, sl, :] for i in range(n_plain)] + [token_rows(lbp)]
        mx = functools.reduce(jnp.maximum, ms)
        w = [jnp.exp2(m - mx) for m in ms]
        num = sum(wi * a for wi, a in zip(w, accs))
        den = sum(wi * l for wi, l in zip(w, ls))
        o_ref[0, sl, :] = (num / den).astype(o_ref.dtype)
        return carry

    lax.fori_loop(0, seq // chunk, combine, 0)


def _attention(q, k, v):
    b, s, _ = q.shape
    geometry = [_branch_geometry(s, d) for d in DILATIONS]
    pad = max(-koff * d for d, (_, _, koff, _) in zip(DILATIONS, geometry))
    whole = max(tq for tq, _, _, nblk in geometry if nblk == 1)
    nk_max = max(nk for _, nk, _, _ in geometry)
    n_plain = len(DILATIONS) - 1
    padded_rows = s // DILATIONS[-1] * _padded_pitch(DILATIONS[-1])
    spec = pl.BlockSpec((1, s, LANES), lambda i, j: (i, 0, j))
    return pl.pallas_call(
        functools.partial(_attn_kernel, seq=s, pad=pad),
        grid=(b, ATTN_WIDTH // LANES),
        in_specs=[spec, spec, spec],
        out_specs=spec,
        out_shape=jax.ShapeDtypeStruct((b, s, ATTN_WIDTH), BF16),
        scratch_shapes=[
            pltpu.VMEM((s, LANES), F32),
            pltpu.VMEM((s + 2 * pad, LANES), F32),
            pltpu.VMEM((s + 2 * pad, LANES), F32),
            pltpu.VMEM((n_plain, s, LANES), F32),
            pltpu.VMEM((n_plain, s, LANES), F32),
            pltpu.VMEM((n_plain, s, LANES), F32),
        ] + [pltpu.VMEM((padded_rows, LANES), F32)] * 6 + [
            pltpu.VMEM((3, TQ, NK), F32),
            pltpu.VMEM((whole, whole), F32),
            pltpu.VMEM((2 * GROUP_ROWS, nk_max), BF16),
            pltpu.VMEM((2 * GROUP_ROWS, nk_max), BF16),
        ],
        compiler_params=pltpu.CompilerParams(
            dimension_semantics=("parallel", "parallel"), vmem_limit_bytes=56 * 2 ** 20),
        name="attn",
    )(q, k, v)


def _post_kernel(a_ref, u_ref, up_ref, un_ref, gb_ref, x_ref, cw_ref, ga_ref, gc_ref, wo_ref,
                 gp_ref, g1_ref, wgu_ref, wd_ref, g2_ref, o_ref, *, tiles_per_seq):
    t = pl.program_id(0) % tiles_per_seq
    u = u_ref[...].astype(F32)
    tm = u.shape[0]
    prev_row = jnp.where(t == 0, 0.0, up_ref[7:8, :].astype(F32))
    next_row = jnp.where(t == tiles_per_seq - 1, 0.0, un_ref[0:1, :].astype(F32))
    row = lax.broadcasted_iota(jnp.int32, u.shape, 0)
    u_prev = jnp.where(row == 0, prev_row, pltpu.roll(u, 1, 0))
    u_next = jnp.where(row == tm - 1, next_row, pltpu.roll(u, tm - 1, 0))
    y = cw_ref[0:1, :] * u_prev + cw_ref[1:2, :] * u + cw_ref[2:3, :] * u_next
    conv = gb_ref[...].astype(F32) * y
    slab = tm // POST_SLABS
    slabs = [slice(j * slab, (j + 1) * slab) for j in range(POST_SLABS)]

    def mix_slab(sl):
        na = _rms(a_ref[sl, :].astype(F32), ga_ref[...]).astype(BF16)
        nc = _rms(conv[sl], gc_ref[...]).astype(BF16)
        mix = (jnp.dot(na, wo_ref[0:ATTN_WIDTH, :], preferred_element_type=F32)
               + jnp.dot(nc, wo_ref[ATTN_WIDTH:, :], preferred_element_type=F32))
        return x_ref[sl, :] + _rms(mix, gp_ref[...])

    def ffn_chunk(state, c):
        if "h" not in state:
            state["h"] = _rms(state["x"], g1_ref[...]).astype(BF16)
        h = state["h"]
        start, size = FFN_CHUNKS[c]
        g = jnp.dot(h, wgu_ref[:, start:start + size], preferred_element_type=F32)
        up = jnp.dot(h, wgu_ref[:, FFN_HIDDEN + start:FFN_HIDDEN + start + size],
                     preferred_element_type=F32)
        act = (g * jax.nn.sigmoid(g) * up).astype(BF16)
        part = jnp.dot(act, wd_ref[start:start + size, :], preferred_element_type=F32)
        state["f"] = part if "f" not in state else state["f"] + part
        if c == len(FFN_CHUNKS) - 1:
            o_ref[state["sl"], :] = state["x"] + _rms(state["f"], g2_ref[...])

    states = {}
    for op, j in POST_ORDER:
        if op == "mix":
            states[j] = {"sl": slabs[j], "x": mix_slab(slabs[j])}
        else:
            ffn_chunk(states[j], op)


def _post(attn2, u, gb, x2, conv_w, g_attn, g_conv, w_out, g_post, g_pre, w_gu, w_down, g_post2,
          layer, seq):
    n = x2.shape[0]
    row = lambda i: (i, 0)
    halo = POST_TILE // 8
    vec = lambda width: pl.BlockSpec((None, 1, width), lambda i: (layer, 0, 0))
    resident = lambda r, c: pl.BlockSpec((r, c), lambda i: (0, 0), pipeline_mode=pl.Buffered(1))
    return pl.pallas_call(
        functools.partial(_post_kernel, tiles_per_seq=seq // POST_TILE),
        grid=(n // POST_TILE,),
        in_specs=[
            pl.BlockSpec((POST_TILE, ATTN_WIDTH), row),
            pl.BlockSpec((POST_TILE, CONV_WIDTH), row),
            pl.BlockSpec((8, CONV_WIDTH), lambda i: (jnp.maximum(i * halo - 1, 0), 0)),
            pl.BlockSpec((8, CONV_WIDTH), lambda i: (jnp.minimum((i + 1) * halo, n // 8 - 1), 0)),
            pl.BlockSpec((POST_TILE, CONV_WIDTH), row),
            pl.BlockSpec((POST_TILE, D_MODEL), row),
            pl.BlockSpec((None, CONV_K, CONV_WIDTH), lambda i: (layer, 0, 0)),
            vec(ATTN_WIDTH),
            vec(CONV_WIDTH),
            resident(D_MODEL, D_MODEL),
            vec(D_MODEL),
            vec(D_MODEL),
            resident(D_MODEL, 2 * FFN_HIDDEN),
            resident(FFN_HIDDEN, D_MODEL),
            vec(D_MODEL),
        ],
        out_specs=pl.BlockSpec((POST_TILE, D_MODEL), row),
        out_shape=jax.ShapeDtypeStruct((n, D_MODEL), F32),
        compiler_params=pltpu.CompilerParams(
            dimension_semantics=("parallel",), vmem_limit_bytes=58 * 2 ** 20),
        name="post",
    )(attn2, u, u, u, gb, x2, conv_w, g_attn, g_conv, w_out, g_post, g_pre, w_gu, w_down, g_post2)


def _rotary_tables(positions):
    half = ROPE_DIM // 2
    inv_freq = ROPE_THETA ** (-jnp.arange(0, ROPE_DIM, 2, dtype=F32) / ROPE_DIM)
    in_head = jnp.arange(LANES) % HEAD_DIM
    ang = positions.astype(F32).reshape(-1, 1) * inv_freq
    cos = jnp.tile(jnp.cos(ang), (1, LANES // half))
    sin = jnp.tile(jnp.sin(ang), (1, LANES // half))
    c = jnp.where(in_head < ROPE_DIM, cos, 1.0)
    s_up = jnp.where((in_head >= half) & (in_head < ROPE_DIM), sin, 0.0)
    s_dn = jnp.where(in_head < half, -sin, 0.0)
    return c, s_up, s_dn


def kernel(x, positions, pre_mix_norm, w_in, conv_w, attn_out_norm, conv_out_norm, w_out,
           post_mix_norm, pre_ffn_norm, w_gate_up, w_down, post_ffn_norm):
    b, s, d = x.shape
    depth = w_in.shape[0]
    n = b * s
    cos, sin_up, sin_dn = _rotary_tables(positions)
    vec = lambda g: g.reshape(depth, 1, g.shape[-1])
    w_in_b = w_in[0].astype(BF16)
    x2 = x.reshape(n, d)
    for l in range(depth):
        casts = [(w_out, l), (w_gate_up, l), (w_down, l)] + ([(w_in, l + 1)] if l + 1 < depth else [])
        (q, k, v, u, gb), cast = _in_proj(x2, vec(pre_mix_norm), w_in_b, cos, sin_up, sin_dn, l,
                                          casts)
        w_out_b, w_gu_b, w_down_b = cast[:3]
        w_in_b = cast[3] if l + 1 < depth else None
        attn = _attention(q.reshape(b, s, ATTN_WIDTH), k.reshape(b, s, ATTN_WIDTH),
                          v.reshape(b, s, ATTN_WIDTH))
        x2 = _post(attn.reshape(n, ATTN_WIDTH), u, gb, x2, conv_w, vec(attn_out_norm),
                   vec(conv_out_norm), w_out_b, vec(post_mix_norm), vec(pre_ffn_norm), w_gu_b,
                   w_down_b, vec(post_ffn_norm), l, s)
    return x2.reshape(b, s, d)
```

```python
import functools

import jax
import jax.numpy as jnp
from jax import lax
from jax.experimental import pallas as pl
from jax.experimental.pallas import tpu as pltpu

D_MODEL = 1024
HEAD_DIM = 64
ATTN_WIDTH = 512
CONV_WIDTH = 512
N_ATTN_HEADS = 8
IN_PROJ_WIDTH = 3 * ATTN_WIDTH + 3 * CONV_WIDTH
CONV_K = 3
ROPE_DIM = 16
ROPE_THETA = 500000.0
DILATIONS = (1, 4, 16)
HALF = 64
FFN_HIDDEN = 2816
RMS_EPS = 1e-6
NEG_INF = -1e30
LOG2_E = 1.4426950408889634

LANES = 128
ROW_TILE = 1024
TQ = 128
NK = TQ + 2 * HALF
GROUP_ROWS = 4 * TQ
VALUES_FIRST_NUM, VALUES_FIRST_DEN = 0, 2
FFN_CHUNKS = ((0, 1536), (1536, 1280))
POST_TILE = 1024
POST_SLABS = 4
POST_ORDER = (tuple(("mix", j) for j in range(POST_SLABS))
              + tuple((c, j) for j in range(POST_SLABS) for c in range(len(FFN_CHUNKS))))

F32 = jnp.float32
BF16 = jnp.bfloat16


def _rms(x, g):
    return x * lax.rsqrt(jnp.mean(x * x, axis=-1, keepdims=True) + RMS_EPS) * g


def _in_proj_kernel(x_ref, g_ref, w_ref, c_ref, s1_ref, s2_ref, *refs):
    n_cast = (len(refs) - 5) // 2
    cast_src = refs[:n_cast]
    q_ref, k_ref, v_ref, u_ref, gb_ref = refs[n_cast:n_cast + 5]
    cast_dst = refs[n_cast + 5:]
    h = _rms(x_ref[...], g_ref[...]).astype(BF16)

    def proj(c):
        return jnp.dot(h, w_ref[:, c * 512:(c + 1) * 512], preferred_element_type=F32)

    cos, sin_up, sin_dn = c_ref[...], s1_ref[...], s2_ref[...]

    def rotary(t):
        parts = []
        for j in range(ATTN_WIDTH // LANES):
            tj = t[:, j * LANES:(j + 1) * LANES]
            parts.append(tj * cos + pltpu.roll(tj, 8, 1) * sin_up
                         + pltpu.roll(tj, LANES - 8, 1) * sin_dn)
        return jnp.concatenate(parts, axis=1)

    q_ref[...] = (rotary(proj(0)) * (HEAD_DIM ** -0.5 * LOG2_E)).astype(BF16)
    k_ref[...] = rotary(proj(1)).astype(BF16)
    v_ref[...] = proj(2).astype(BF16)
    u_ref[...] = (proj(5) * proj(3)).astype(BF16)
    gb_ref[...] = proj(4).astype(BF16)
    for src, dst in zip(cast_src, cast_dst):
        dst[...] = src[...].astype(BF16)


def _in_proj(x2, g, w, cos, sin_up, sin_dn, layer, casts):
    n = x2.shape[0]
    steps = n // ROW_TILE
    row = lambda i: (i, 0)
    out = jax.ShapeDtypeStruct((n, 512), BF16)
    cast_in_specs, cast_out_specs, cast_shapes = [], [], []
    for wt, lyr in casts:
        r, c = wt.shape[1:]
        cast_in_specs.append(pl.BlockSpec((None, r // steps, c), lambda i, lyr=lyr: (lyr, i, 0)))
        cast_out_specs.append(pl.BlockSpec((r // steps, c), row))
        cast_shapes.append(jax.ShapeDtypeStruct((r, c), BF16))
    res = pl.pallas_call(
        _in_proj_kernel,
        grid=(steps,),
        in_specs=[
            pl.BlockSpec((ROW_TILE, D_MODEL), row),
            pl.BlockSpec((None, 1, D_MODEL), lambda i: (layer, 0, 0)),
            pl.BlockSpec((D_MODEL, IN_PROJ_WIDTH), lambda i: (0, 0), pipeline_mode=pl.Buffered(1)),
            pl.BlockSpec((ROW_TILE, LANES), row),
            pl.BlockSpec((ROW_TILE, LANES), row),
            pl.BlockSpec((ROW_TILE, LANES), row),
        ] + cast_in_specs,
        out_specs=[pl.BlockSpec((ROW_TILE, 512), row)] * 5 + cast_out_specs,
        out_shape=[out] * 5 + cast_shapes,
        compiler_params=pltpu.CompilerParams(
            dimension_semantics=("parallel",), vmem_limit_bytes=48 * 2 ** 20),
        name="in_proj",
    )(x2, g, w, cos, sin_up, sin_dn, *[wt for wt, _ in casts])
    return res[:5], res[5:]


def _branch_geometry(seq, d):
    length = seq // d
    if length <= 2 * TQ:
        return length, length, 0, 1
    return TQ, NK, -HALF, length // TQ


def _padded_pitch(d):
    return d + d // 2


def _attn_kernel(q_ref, k_ref, v_ref, o_ref, qf, kf, vf, ob, lb, mb, qp, kp, vp, obp, lbp, mbp,
                 bias, bias_whole, pbuf_even, pbuf_odd, *, seq, pad):
    pbuf = (pbuf_even, pbuf_odd)
    geometry = [_branch_geometry(seq, d) for d in DILATIONS]
    n_plain = sum(1 for g in geometry if g[3] > 1)
    assert all(g[3] > 1 for g in geometry[:n_plain]) and len(geometry) == n_plain + 1
    d_whole = DILATIONS[-1]
    pitch = _padded_pitch(d_whole)
    head_a = {tq: lax.broadcasted_iota(jnp.int32, (tq, LANES), 1) < HEAD_DIM
              for tq in {g[0] for g in geometry}}

    qf[...] = q_ref[0].astype(F32)
    zeros = jnp.zeros((pad, LANES), F32)
    for src, dst in ((k_ref, kf), (v_ref, vf)):
        dst[0:pad, :] = zeros
        dst[pad + seq:pad + seq + pad, :] = zeros
        dst[pad:pad + seq, :] = src[0].astype(F32)

    def pad_copies():
        for src, dst in ((q_ref, qp), (k_ref, kp), (v_ref, vp)):
            for g in range(seq // d_whole):
                dst[g * pitch:g * pitch + d_whole, :] = (
                    src[0, g * d_whole:(g + 1) * d_whole, :].astype(F32))

    r = lax.broadcasted_iota(jnp.int32, (TQ, NK), 0)
    c = lax.broadcasted_iota(jnp.int32, (TQ, NK), 1)
    band = (c >= r) & (c <= r + 2 * HALF)
    bias[0] = jnp.where(band & (c >= HALF), 0.0, NEG_INF)
    bias[1] = jnp.where(band, 0.0, NEG_INF)
    bias[2] = jnp.where(band & (c < HALF + TQ), 0.0, NEG_INF)
    rw = lax.broadcasted_iota(jnp.int32, bias_whole.shape, 0)
    cw = lax.broadcasted_iota(jnp.int32, bias_whole.shape, 1)
    bias_whole[...] = jnp.where(jnp.abs(rw - cw) <= HALF, 0.0, NEG_INF)

    def rows(start, size, d):
        return pl.ds(start, size) if d == 1 else pl.ds(start, size, stride=d)

    n_groups = seq // GROUP_ROWS

    def group_blocks(br, t):
        tq, _, _, nblk = geometry[br]
        per_group = GROUP_ROWS // tq
        if nblk >= per_group:
            ngrp = nblk // per_group
            res, g = (t, 0) if ngrp == 1 else (t // ngrp, t % ngrp)
            static = isinstance(t, int) or ngrp == 1
            out = []
            for j in range(per_group):
                idx = 1
                if j == 0:
                    idx = (0 if g == 0 else 1) if static else jnp.where(g == 0, 0, 1)
                elif j == per_group - 1:
                    idx = (2 if g == ngrp - 1 else 1) if static else jnp.where(g == ngrp - 1, 2, 1)
                out.append((g * per_group + j, res, idx))
            return out
        per = per_group // nblk
        return [(i, t * per + jr, 0 if i == 0 else (2 if i == nblk - 1 else 1))
                for jr in range(per) for i in range(nblk)]

    def score_block(br, slot, n, blk):
        d = DILATIONS[br]
        tq, nk, koff, nblk = geometry[br]
        i, res, bias_idx = blk
        q0 = res + d * tq * i
        k0 = pad + res + d * (tq * i + koff)
        if nblk == 1:
            qb = qp[pl.ds(res, tq, stride=pitch), :]
            kb = kp[pl.ds(res, nk, stride=pitch), :].astype(BF16)
            b = bias_whole[...]
        else:
            qb = qf[rows(q0, tq, d), :]
            kb = kf[rows(k0, nk, d), :].astype(BF16)
            b = bias[bias_idx]
        q2 = jnp.concatenate([jnp.where(head_a[tq], qb, 0.0), jnp.where(head_a[tq], 0.0, qb)],
                             axis=0).astype(BF16)
        s2 = lax.dot_general(q2, kb, (((1,), (1,)), ((), ())), preferred_element_type=F32)
        ms = []
        for h in range(2):
            s = s2[h * tq:(h + 1) * tq] + b
            m = jnp.max(s, axis=1, keepdims=True)
            pbuf[slot][pl.ds((2 * n + h) * tq, tq), 0:nk] = jnp.exp2(s - m).astype(BF16)
            ms.append(m)
        m2 = jnp.where(head_a[tq], ms[0], ms[1])
        if nblk == 1:
            mbp[pl.ds(res, tq, stride=pitch), :] = m2
        else:
            mb[br, rows(q0, tq, d), :] = m2

    def value_block(br, slot, n, blk):
        d = DILATIONS[br]
        tq, nk, koff, nblk = geometry[br]
        i, res, _ = blk
        q0 = res + d * tq * i
        k0 = pad + res + d * (tq * i + koff)
        if nblk == 1:
            vb = vp[pl.ds(res, nk, stride=pitch), :].astype(BF16)
        else:
            vb = vf[rows(k0, nk, d), :].astype(BF16)
        r2 = jnp.dot(pbuf[slot][pl.ds(2 * n * tq, 2 * tq), 0:nk],
                     jnp.concatenate([vb, jnp.ones_like(vb)], axis=1),
                     preferred_element_type=F32)
        acc = jnp.where(head_a[tq], r2[:tq, :LANES], r2[tq:, :LANES])
        l = jnp.where(head_a[tq], r2[:tq, LANES:], r2[tq:, LANES:])
        if nblk == 1:
            obp[pl.ds(res, tq, stride=pitch), :] = acc
            lbp[pl.ds(res, tq, stride=pitch), :] = l
        else:
            ob[br, rows(q0, tq, d), :] = acc
            lb[br, rows(q0, tq, d), :] = l

    def stage(prev, cur):
        v = [] if prev is None else [
            functools.partial(value_block, prev[0], prev[2], n, blk)
            for n, blk in enumerate(group_blocks(prev[0], prev[1]))]
        s = [] if cur is None else [
            functools.partial(score_block, cur[0], cur[2], n, blk)
            for n, blk in enumerate(group_blocks(cur[0], cur[1]))]
        head = len(v) * VALUES_FIRST_NUM // VALUES_FIRST_DEN
        for thunk in v[:head] + s + v[head:]:
            thunk()

    assert n_groups % 2 == 0
    last = n_groups - 1
    stage(None, (0, 0, 0))
    for br in range(len(DILATIONS)):
        for t in range(1, n_groups):
            stage((br, t - 1, (t - 1) % 2), (br, t, t % 2))
        stage((br, last, last % 2), (br + 1, 0, 0) if br + 1 < len(DILATIONS) else None)
        if br == 0:
            pad_copies()

    chunk = 256

    padded_chunk = chunk // d_whole * pitch

    def combine(j, carry):
        sl = pl.ds(pl.multiple_of(j * chunk, chunk), chunk)
        slp = pl.ds(pl.multiple_of(j * padded_chunk, 8), padded_chunk)

        def token_rows(ref):
            x = ref[slp, :].reshape(chunk // d_whole, pitch, LANES)
            return x[:, :d_whole, :].reshape(chunk, LANES)

        ms = [mb[i, sl, :] for i in range(n_plain)] + [token_rows(mbp)]
        accs = [ob[i, sl, :] for i in range(n_plain)] + [token_rows(obp)]
        ls = [lb[i, sl, :] for i in range(n_plain)] + [token_rows(lbp)]
        mx = functools.reduce(jnp.maximum, ms)
        w = [jnp.exp2(m - mx) for m in ms]
        num = sum(wi * a for wi, a in zip(w, accs))
        den = sum(wi * l for wi, l in zip(w, ls))
        o_ref[0, sl, :] = (num / den).astype(o_ref.dtype)
        return carry

    lax.fori_loop(0, seq // chunk, combine, 0)


def _attention(q, k, v):
    b, s, _ = q.shape
    geometry = [_branch_geometry(s, d) for d in DILATIONS]
    pad = max(-koff * d for d, (_, _, koff, _) in zip(DILATIONS, geometry))
    whole = max(tq for tq, _, _, nblk in geometry if nblk == 1)
    nk_max = max(nk for _, nk, _, _ in geometry)
    n_plain = len(DILATIONS) - 1
    padded_rows = s // DILATIONS[-1] * _padded_pitch(DILATIONS[-1])
    spec = pl.BlockSpec((1, s, LANES), lambda i, j: (i, 0, j))
    return pl.pallas_call(
        functools.partial(_attn_kernel, seq=s, pad=pad),
        grid=(b, ATTN_WIDTH // LANES),
        in_specs=[spec, spec, spec],
        out_specs=spec,
        out_shape=jax.ShapeDtypeStruct((b, s, ATTN_WIDTH), BF16),
        scratch_shapes=[
            pltpu.VMEM((s, LANES), F32),
            pltpu.VMEM((s + 2 * pad, LANES), F32),
            pltpu.VMEM((s + 2 * pad, LANES), F32),
            pltpu.VMEM((n_plain, s, LANES), F32),
            pltpu.VMEM((n_plain, s, LANES), F32),
            pltpu.VMEM((n_plain, s, LANES), F32),
        ] + [pltpu.VMEM((padded_rows, LANES), F32)] * 6 + [
            pltpu.VMEM((3, TQ, NK), F32),
            pltpu.VMEM((whole, whole), F32),
            pltpu.VMEM((2 * GROUP_ROWS, nk_max), BF16),
            pltpu.VMEM((2 * GROUP_ROWS, nk_max), BF16),
        ],
        compiler_params=pltpu.CompilerParams(
            dimension_semantics=("parallel", "parallel"), vmem_limit_bytes=56 * 2 ** 20),
        name="attn",
    )(q, k, v)


def _post_kernel(a_ref, u_ref, up_ref, un_ref, gb_ref, x_ref, cw_ref, ga_ref, gc_ref, wo_ref,
                 gp_ref, g1_ref, wgu_ref, wd_ref, g2_ref, o_ref, *, tiles_per_seq):
    t = pl.program_id(0) % tiles_per_seq
    u = u_ref[...].astype(F32)
    tm = u.shape[0]
    prev_row = jnp.where(t == 0, 0.0, up_ref[7:8, :].astype(F32))
    next_row = jnp.where(t == tiles_per_seq - 1, 0.0, un_ref[0:1, :].astype(F32))
    row = lax.broadcasted_iota(jnp.int32, u.shape, 0)
    u_prev = jnp.where(row == 0, prev_row, pltpu.roll(u, 1, 0))
    u_next = jnp.where(row == tm - 1, next_row, pltpu.roll(u, tm - 1, 0))
    y = cw_ref[0:1, :] * u_prev + cw_ref[1:2, :] * u + cw_ref[2:3, :] * u_next
    conv = gb_ref[...].astype(F32) * y
    slab = tm // POST_SLABS
    slabs = [slice(j * slab, (j + 1) * slab) for j in range(POST_SLABS)]

    def mix_slab(sl):
        na = _rms(a_ref[sl, :].astype(F32), ga_ref[...]).astype(BF16)
        nc = _rms(conv[sl], gc_ref[...]).astype(BF16)
        mix = (jnp.dot(na, wo_ref[0:ATTN_WIDTH, :], preferred_element_type=F32)
               + jnp.dot(nc, wo_ref[ATTN_WIDTH:, :], preferred_element_type=F32))
        return x_ref[sl, :] + _rms(mix, gp_ref[...])

    def ffn_chunk(state, c):
        if "h" not in state:
            state["h"] = _rms(state["x"], g1_ref[...]).astype(BF16)
        h = state["h"]
        start, size = FFN_CHUNKS[c]
        g = jnp.dot(h, wgu_ref[:, start:start + size], preferred_element_type=F32)
        up = jnp.dot(h, wgu_ref[:, FFN_HIDDEN + start:FFN_HIDDEN + start + size],
                     preferred_element_type=F32)
        act = (g * jax.nn.sigmoid(g) * up).astype(BF16)
        part = jnp.dot(act, wd_ref[start:start + size, :], preferred_element_type=F32)
        state["f"] = part if "f" not in state else state["f"] + part
        if c == len(FFN_CHUNKS) - 1:
            o_ref[state["sl"], :] = state["x"] + _rms(state["f"], g2_ref[...])

    states = {}
    for op, j in POST_ORDER:
        if op == "mix":
            states[j] = {"sl": slabs[j], "x": mix_slab(slabs[j])}
        else:
            ffn_chunk(states[j], op)


def _post(attn2, u, gb, x2, conv_w, g_attn, g_conv, w_out, g_post, g_pre, w_gu, w_down, g_post2,
          layer, seq):
    n = x2.shape[0]
    row = lambda i: (i, 0)
    halo = POST_TILE // 8
    vec = lambda width: pl.BlockSpec((None, 1, width), lambda i: (layer, 0, 0))
    resident = lambda r, c: pl.BlockSpec((r, c), lambda i: (0, 0), pipeline_mode=pl.Buffered(1))
    return pl.pallas_call(
        functools.partial(_post_kernel, tiles_per_seq=seq // POST_TILE),
        grid=(n // POST_TILE,),
        in_specs=[
            pl.BlockSpec((POST_TILE, ATTN_WIDTH), row),
            pl.BlockSpec((POST_TILE, CONV_WIDTH), row),
            pl.BlockSpec((8, CONV_WIDTH), lambda i: (jnp.maximum(i * halo - 1, 0), 0)),
            pl.BlockSpec((8, CONV_WIDTH), lambda i: (jnp.minimum((i + 1) * halo, n // 8 - 1), 0)),
            pl.BlockSpec((POST_TILE, CONV_WIDTH), row),
            pl.BlockSpec((POST_TILE, D_MODEL), row),
            pl.BlockSpec((None, CONV_K, CONV_WIDTH), lambda i: (layer, 0, 0)),
            vec(ATTN_WIDTH),
            vec(CONV_WIDTH),
            resident(D_MODEL, D_MODEL),
            vec(D_MODEL),
            vec(D_MODEL),
            resident(D_MODEL, 2 * FFN_HIDDEN),
            resident(FFN_HIDDEN, D_MODEL),
            vec(D_MODEL),
        ],
        out_specs=pl.BlockSpec((POST_TILE, D_MODEL), row),
        out_shape=jax.ShapeDtypeStruct((n, D_MODEL), F32),
        compiler_params=pltpu.CompilerParams(
            dimension_semantics=("parallel",), vmem_limit_bytes=58 * 2 ** 20),
        name="post",
    )(attn2, u, u, u, gb, x2, conv_w, g_attn, g_conv, w_out, g_post, g_pre, w_gu, w_down, g_post2)


def _rotary_tables(positions):
    half = ROPE_DIM // 2
    inv_freq = ROPE_THETA ** (-jnp.arange(0, ROPE_DIM, 2, dtype=F32) / ROPE_DIM)
    in_head = jnp.arange(LANES) % HEAD_DIM
    ang = positions.astype(F32).reshape(-1, 1) * inv_freq
    cos = jnp.tile(jnp.cos(ang), (1, LANES // half))
    sin = jnp.tile(jnp.sin(ang), (1, LANES // half))
    c = jnp.where(in_head < ROPE_DIM, cos, 1.0)
    s_up = jnp.where((in_head >= half) & (in_head < ROPE_DIM), sin, 0.0)
    s_dn = jnp.where(in_head < half, -sin, 0.0)
    return c, s_up, s_dn


def kernel(x, positions, pre_mix_norm, w_in, conv_w, attn_out_norm, conv_out_norm, w_out,
           post_mix_norm, pre_ffn_norm, w_gate_up, w_down, post_ffn_norm):
    b, s, d = x.shape
    depth = w_in.shape[0]
    n = b * s
    cos, sin_up, sin_dn = _rotary_tables(positions)
    vec = lambda g: g.reshape(depth, 1, g.shape[-1])
    w_in_b = w_in[0].astype(BF16)
    x2 = x.reshape(n, d)
    for l in range(depth):
        casts = [(w_out, l), (w_gate_up, l), (w_down, l)] + ([(w_in, l + 1)] if l + 1 < depth else [])
        (q, k, v, u, gb), cast = _in_proj(x2, vec(pre_mix_norm), w_in_b, cos, sin_up, sin_dn, l,
                                          casts)
        w_out_b, w_gu_b, w_down_b = cast[:3]
        w_in_b = cast[3] if l + 1 < depth else None
        attn = _attention(q.reshape(b, s, ATTN_WIDTH), k.reshape(b, s, ATTN_WIDTH),
                          v.reshape(b, s, ATTN_WIDTH))
        x2 = _post(attn.reshape(n, ATTN_WIDTH), u, gb, x2, conv_w, vec(attn_out_norm),
                   vec(conv_out_norm), w_out_b, vec(post_mix_norm), vec(pre_ffn_norm), w_gu_b,
                   w_down_b, vec(post_ffn_norm), l, s)
    return x2.reshape(b, s, d)
```

```python
import functools

import jax
import jax.numpy as jnp
from jax import lax
from jax.experimental import pallas as pl
from jax.experimental.pallas import tpu as pltpu

D_MODEL = 1024
HEAD_DIM = 64
ATTN_WIDTH = 512
CONV_WIDTH = 512
N_ATTN_HEADS = 8
IN_PROJ_WIDTH = 3 * ATTN_WIDTH + 3 * CONV_WIDTH
CONV_K = 3
ROPE_DIM = 16
ROPE_THETA = 500000.0
DILATIONS = (1, 4, 16)
HALF = 64
FFN_HIDDEN = 2816
RMS_EPS = 1e-6
NEG_INF = -1e30
LOG2_E = 1.4426950408889634

LANES = 128
ROW_TILE = 1024
TQ = 128
NK = TQ + 2 * HALF
GROUP_ROWS = 2 * TQ
VALUES_FIRST_NUM, VALUES_FIRST_DEN = 0, 2
FFN_CHUNKS = ((0, 1536), (1536, 1280))
POST_TILE = 1024
POST_SLABS = 4
POST_ORDER = (tuple(("mix", j) for j in range(POST_SLABS))
              + tuple((c, j) for j in range(POST_SLABS) for c in range(len(FFN_CHUNKS))))

F32 = jnp.float32
BF16 = jnp.bfloat16


def _rms(x, g):
    return x * lax.rsqrt(jnp.mean(x * x, axis=-1, keepdims=True) + RMS_EPS) * g


def _in_proj_kernel(x_ref, g_ref, w_ref, c_ref, s1_ref, s2_ref, *refs):
    n_cast = (len(refs) - 5) // 2
    cast_src = refs[:n_cast]
    q_ref, k_ref, v_ref, u_ref, gb_ref = refs[n_cast:n_cast + 5]
    cast_dst = refs[n_cast + 5:]
    h = _rms(x_ref[...], g_ref[...]).astype(BF16)

    def proj(c):
        return jnp.dot(h, w_ref[:, c * 512:(c + 1) * 512], preferred_element_type=F32)

    cos, sin_up, sin_dn = c_ref[...], s1_ref[...], s2_ref[...]

    def rotary(t):
        parts = []
        for j in range(ATTN_WIDTH // LANES):
            tj = t[:, j * LANES:(j + 1) * LANES]
            parts.append(tj * cos + pltpu.roll(tj, 8, 1) * sin_up
                         + pltpu.roll(tj, LANES - 8, 1) * sin_dn)
        return jnp.concatenate(parts, axis=1)

    q_ref[...] = (rotary(proj(0)) * (HEAD_DIM ** -0.5 * LOG2_E)).astype(BF16)
    k_ref[...] = rotary(proj(1)).astype(BF16)
    v_ref[...] = proj(2).astype(BF16)
    u_ref[...] = (proj(5) * proj(3)).astype(BF16)
    gb_ref[...] = proj(4).astype(BF16)
    for src, dst in zip(cast_src, cast_dst):
        dst[...] = src[...].astype(BF16)


def _in_proj(x2, g, w, cos, sin_up, sin_dn, layer, casts):
    n = x2.shape[0]
    steps = n // ROW_TILE
    row = lambda i: (i, 0)
    out = jax.ShapeDtypeStruct((n, 512), BF16)
    cast_in_specs, cast_out_specs, cast_shapes = [], [], []
    for wt, lyr in casts:
        r, c = wt.shape[1:]
        cast_in_specs.append(pl.BlockSpec((None, r // steps, c), lambda i, lyr=lyr: (lyr, i, 0)))
        cast_out_specs.append(pl.BlockSpec((r // steps, c), row))
        cast_shapes.append(jax.ShapeDtypeStruct((r, c), BF16))
    res = pl.pallas_call(
        _in_proj_kernel,
        grid=(steps,),
        in_specs=[
            pl.BlockSpec((ROW_TILE, D_MODEL), row),
            pl.BlockSpec((None, 1, D_MODEL), lambda i: (layer, 0, 0)),
            pl.BlockSpec((D_MODEL, IN_PROJ_WIDTH), lambda i: (0, 0), pipeline_mode=pl.Buffered(1)),
            pl.BlockSpec((ROW_TILE, LANES), row),
            pl.BlockSpec((ROW_TILE, LANES), row),
            pl.BlockSpec((ROW_TILE, LANES), row),
        ] + cast_in_specs,
        out_specs=[pl.BlockSpec((ROW_TILE, 512), row)] * 5 + cast_out_specs,
        out_shape=[out] * 5 + cast_shapes,
        compiler_params=pltpu.CompilerParams(
            dimension_semantics=("parallel",), vmem_limit_bytes=48 * 2 ** 20),
        name="in_proj",
    )(x2, g, w, cos, sin_up, sin_dn, *[wt for wt, _ in casts])
    return res[:5], res[5:]


def _branch_geometry(seq, d):
    length = seq // d
    if length <= 2 * TQ:
        return length, length, 0, 1
    return TQ, NK, -HALF, length // TQ


def _padded_pitch(d):
    return d + d // 2


def _attn_kernel(q_ref, k_ref, v_ref, o_ref, qf, kf, vf, ob, lb, mb, qp, kp, vp, obp, lbp, mbp,
                 bias, bias_whole, pbuf_even, pbuf_odd, *, seq, pad):
    pbuf = (pbuf_even, pbuf_odd)
    geometry = [_branch_geometry(seq, d) for d in DILATIONS]
    n_plain = sum(1 for g in geometry if g[3] > 1)
    assert all(g[3] > 1 for g in geometry[:n_plain]) and len(geometry) == n_plain + 1
    d_whole = DILATIONS[-1]
    pitch = _padded_pitch(d_whole)
    head_a = {tq: lax.broadcasted_iota(jnp.int32, (tq, LANES), 1) < HEAD_DIM
              for tq in {g[0] for g in geometry}}

    qf[...] = q_ref[0].astype(F32)
    zeros = jnp.zeros((pad, LANES), F32)
    for src, dst in ((k_ref, kf), (v_ref, vf)):
        dst[0:pad, :] = zeros
        dst[pad + seq:pad + seq + pad, :] = zeros
        dst[pad:pad + seq, :] = src[0].astype(F32)

    def pad_copies():
        for src, dst in ((q_ref, qp), (k_ref, kp), (v_ref, vp)):
            for g in range(seq // d_whole):
                dst[g * pitch:g * pitch + d_whole, :] = (
                    src[0, g * d_whole:(g + 1) * d_whole, :].astype(F32))

    r = lax.broadcasted_iota(jnp.int32, (TQ, NK), 0)
    c = lax.broadcasted_iota(jnp.int32, (TQ, NK), 1)
    band = (c >= r) & (c <= r + 2 * HALF)
    bias[0] = jnp.where(band & (c >= HALF), 0.0, NEG_INF)
    bias[1] = jnp.where(band, 0.0, NEG_INF)
    bias[2] = jnp.where(band & (c < HALF + TQ), 0.0, NEG_INF)
    rw = lax.broadcasted_iota(jnp.int32, bias_whole.shape, 0)
    cw = lax.broadcasted_iota(jnp.int32, bias_whole.shape, 1)
    bias_whole[...] = jnp.where(jnp.abs(rw - cw) <= HALF, 0.0, NEG_INF)

    def rows(start, size, d):
        return pl.ds(start, size) if d == 1 else pl.ds(start, size, stride=d)

    n_groups = seq // GROUP_ROWS

    def group_blocks(br, t):
        tq, _, _, nblk = geometry[br]
        per_group = GROUP_ROWS // tq
        if nblk >= per_group:
            ngrp = nblk // per_group
            res, g = (t, 0) if ngrp == 1 else (t // ngrp, t % ngrp)
            static = isinstance(t, int) or ngrp == 1
            out = []
            for j in range(per_group):
                idx = 1
                if j == 0:
                    idx = (0 if g == 0 else 1) if static else jnp.where(g == 0, 0, 1)
                elif j == per_group - 1:
                    idx = (2 if g == ngrp - 1 else 1) if static else jnp.where(g == ngrp - 1, 2, 1)
                out.append((g * per_group + j, res, idx))
            return out
        per = per_group // nblk
        return [(i, t * per + jr, 0 if i == 0 else (2 if i == nblk - 1 else 1))
                for jr in range(per) for i in range(nblk)]

    def score_block(br, slot, n, blk):
        d = DILATIONS[br]
        tq, nk, koff, nblk = geometry[br]
        i, res, bias_idx = blk
        q0 = res + d * tq * i
        k0 = pad + res + d * (tq * i + koff)
        if nblk == 1:
            qb = qp[pl.ds(res, tq, stride=pitch), :]
            kb = kp[pl.ds(res, nk, stride=pitch), :].astype(BF16)
            b = bias_whole[...]
        else:
            qb = qf[rows(q0, tq, d), :]
            kb = kf[rows(k0, nk, d), :].astype(BF16)
            b = bias[bias_idx]
        q2 = jnp.concatenate([jnp.where(head_a[tq], qb, 0.0), jnp.where(head_a[tq], 0.0, qb)],
                             axis=0).astype(BF16)
        s2 = lax.dot_general(q2, kb, (((1,), (1,)), ((), ())), preferred_element_type=F32)
        ms = []
        for h in range(2):
            s = s2[h * tq:(h + 1) * tq] + b
            m = jnp.max(s, axis=1, keepdims=True)
            pbuf[slot][pl.ds((2 * n + h) * tq, tq), 0:nk] = jnp.exp2(s - m).astype(BF16)
            ms.append(m)
        m2 = jnp.where(head_a[tq], ms[0], ms[1])
        if nblk == 1:
            mbp[pl.ds(res, tq, stride=pitch), :] = m2
        else:
            mb[br, rows(q0, tq, d), :] = m2

    def value_block(br, slot, n, blk):
        d = DILATIONS[br]
        tq, nk, koff, nblk = geometry[br]
        i, res, _ = blk
        q0 = res + d * tq * i
        k0 = pad + res + d * (tq * i + koff)
        if nblk == 1:
            vb = vp[pl.ds(res, nk, stride=pitch), :].astype(BF16)
        else:
            vb = vf[rows(k0, nk, d), :].astype(BF16)
        r2 = jnp.dot(pbuf[slot][pl.ds(2 * n * tq, 2 * tq), 0:nk],
                     jnp.concatenate([vb, jnp.ones_like(vb)], axis=1),
                     preferred_element_type=F32)
        acc = jnp.where(head_a[tq], r2[:tq, :LANES], r2[tq:, :LANES])
        l = jnp.where(head_a[tq], r2[:tq, LANES:], r2[tq:, LANES:])
        if nblk == 1:
            obp[pl.ds(res, tq, stride=pitch), :] = acc
            lbp[pl.ds(res, tq, stride=pitch), :] = l
        else:
            ob[br, rows(q0, tq, d), :] = acc
            lb[br, rows(q0, tq, d), :] = l

    def stage(prev, cur):
        v = [] if prev is None else [
            functools.partial(value_block, prev[0], prev[2], n, blk)
            for n, blk in enumerate(group_blocks(prev[0], prev[1]))]
        s = [] if cur is None else [
            functools.partial(score_block, cur[0], cur[2], n, blk)
            for n, blk in enumerate(group_blocks(cur[0], cur[1]))]
        head = len(v) * VALUES_FIRST_NUM // VALUES_FIRST_DEN
        for thunk in v[:head] + s + v[head:]:
            thunk()

    assert n_groups % 2 == 0
    last = n_groups - 1
    stage(None, (0, 0, 0))
    for br in range(len(DILATIONS)):
        for t in range(1, n_groups):
            stage((br, t - 1, (t - 1) % 2), (br, t, t % 2))
        stage((br, last, last % 2), (br + 1, 0, 0) if br + 1 < len(DILATIONS) else None)
        if br == 0:
            pad_copies()

    chunk = 256

    padded_chunk = chunk // d_whole * pitch

    def combine(j, carry):
        sl = pl.ds(pl.multiple_of(j * chunk, chunk), chunk)
        slp = pl.ds(pl.multiple_of(j * padded_chunk, 8), padded_chunk)

        def token_rows(ref):
            x = ref[slp, :].reshape(chunk // d_whole, pitch, LANES)
            return x[:, :d_whole, :].reshape(chunk, LANES)

        ms = [mb[i, sl, :] for i in range(n_plain)] + [token_rows(mbp)]
        accs = [ob[i, sl, :] for i in range(n_plain)] + [token_rows(obp)]
        ls = [lb[i, sl, :] for i in range(n_plain)] + [token_rows(lbp)]
        mx = functools.reduce(jnp.maximum, ms)
        w = [jnp.exp2(m - mx) for m in ms]
        num = sum(wi * a for wi, a in zip(w, accs))
        den = sum(wi * l for wi, l in zip(w, ls))
        o_ref[0, sl, :] = (num / den).astype(o_ref.dtype)
        return carry

    lax.fori_loop(0, seq // chunk, combine, 0)


def _attention(q, k, v):
    b, s, _ = q.shape
    geometry = [_branch_geometry(s, d) for d in DILATIONS]
    pad = max(-koff * d for d, (_, _, koff, _) in zip(DILATIONS, geometry))
    whole = max(tq for tq, _, _, nblk in geometry if nblk == 1)
    nk_max = max(nk for _, nk, _, _ in geometry)
    n_plain = len(DILATIONS) - 1
    padded_rows = s // DILATIONS[-1] * _padded_pitch(DILATIONS[-1])
    spec = pl.BlockSpec((1, s, LANES), lambda i, j: (i, 0, j))
    return pl.pallas_call(
        functools.partial(_attn_kernel, seq=s, pad=pad),
        grid=(b, ATTN_WIDTH // LANES),
        in_specs=[spec, spec, spec],
        out_specs=spec,
        out_shape=jax.ShapeDtypeStruct((b, s, ATTN_WIDTH), BF16),
        scratch_shapes=[
            pltpu.VMEM((s, LANES), F32),
            pltpu.VMEM((s + 2 * pad, LANES), F32),
            pltpu.VMEM((s + 2 * pad, LANES), F32),
            pltpu.VMEM((n_plain, s, LANES), F32),
            pltpu.VMEM((n_plain, s, LANES), F32),
            pltpu.VMEM((n_plain, s, LANES), F32),
        ] + [pltpu.VMEM((padded_rows, LANES), F32)] * 6 + [
            pltpu.VMEM((3, TQ, NK), F32),
            pltpu.VMEM((whole, whole), F32),
            pltpu.VMEM((2 * GROUP_ROWS, nk_max), BF16),
            pltpu.VMEM((2 * GROUP_ROWS, nk_max), BF16),
        ],
        compiler_params=pltpu.CompilerParams(
            dimension_semantics=("parallel", "parallel"), vmem_limit_bytes=56 * 2 ** 20),
        name="attn",
    )(q, k, v)


def _post_kernel(a_ref, u_ref, up_ref, un_ref, gb_ref, x_ref, cw_ref, ga_ref, gc_ref, wo_ref,
                 gp_ref, g1_ref, wgu_ref, wd_ref, g2_ref, o_ref, *, tiles_per_seq):
    t = pl.program_id(0) % tiles_per_seq
    u = u_ref[...].astype(F32)
    tm = u.shape[0]
    prev_row = jnp.where(t == 0, 0.0, up_ref[7:8, :].astype(F32))
    next_row = jnp.where(t == tiles_per_seq - 1, 0.0, un_ref[0:1, :].astype(F32))
    row = lax.broadcasted_iota(jnp.int32, u.shape, 0)
    u_prev = jnp.where(row == 0, prev_row, pltpu.roll(u, 1, 0))
    u_next = jnp.where(row == tm - 1, next_row, pltpu.roll(u, tm - 1, 0))
    y = cw_ref[0:1, :] * u_prev + cw_ref[1:2, :] * u + cw_ref[2:3, :] * u_next
    conv = gb_ref[...].astype(F32) * y
    slab = tm // POST_SLABS
    slabs = [slice(j * slab, (j + 1) * slab) for j in range(POST_SLABS)]

    def mix_slab(sl):
        na = _rms(a_ref[sl, :].astype(F32), ga_ref[...]).astype(BF16)
        nc = _rms(conv[sl], gc_ref[...]).astype(BF16)
        mix = (jnp.dot(na, wo_ref[0:ATTN_WIDTH, :], preferred_element_type=F32)
               + jnp.dot(nc, wo_ref[ATTN_WIDTH:, :], preferred_element_type=F32))
        return x_ref[sl, :] + _rms(mix, gp_ref[...])

    def ffn_chunk(state, c):
        if "h" not in state:
            state["h"] = _rms(state["x"], g1_ref[...]).astype(BF16)
        h = state["h"]
        start, size = FFN_CHUNKS[c]
        g = jnp.dot(h, wgu_ref[:, start:start + size], preferred_element_type=F32)
        up = jnp.dot(h, wgu_ref[:, FFN_HIDDEN + start:FFN_HIDDEN + start + size],
                     preferred_element_type=F32)
        act = (g * jax.nn.sigmoid(g) * up).astype(BF16)
        part = jnp.dot(act, wd_ref[start:start + size, :], preferred_element_type=F32)
        state["f"] = part if "f" not in state else state["f"] + part
        if c == len(FFN_CHUNKS) - 1:
            o_ref[state["sl"], :] = state["x"] + _rms(state["f"], g2_ref[...])

    states = {}
    for op, j in POST_ORDER:
        if op == "mix":
            states[j] = {"sl": slabs[j], "x": mix_slab(slabs[j])}
        else:
            ffn_chunk(states[j], op)


def _post(attn2, u, gb, x2, conv_w, g_attn, g_conv, w_out, g_post, g_pre, w_gu, w_down, g_post2,
          layer, seq):
    n = x2.shape[0]
    row = lambda i: (i, 0)
    halo = POST_TILE // 8
    vec = lambda width: pl.BlockSpec((None, 1, width), lambda i: (layer, 0, 0))
    resident = lambda r, c: pl.BlockSpec((r, c), lambda i: (0, 0), pipeline_mode=pl.Buffered(1))
    return pl.pallas_call(
        functools.partial(_post_kernel, tiles_per_seq=seq // POST_TILE),
        grid=(n // POST_TILE,),
        in_specs=[
            pl.BlockSpec((POST_TILE, ATTN_WIDTH), row),
            pl.BlockSpec((POST_TILE, CONV_WIDTH), row),
            pl.BlockSpec((8, CONV_WIDTH), lambda i: (jnp.maximum(i * halo - 1, 0), 0)),
            pl.BlockSpec((8, CONV_WIDTH), lambda i: (jnp.minimum((i + 1) * halo, n // 8 - 1), 0)),
            pl.BlockSpec((POST_TILE, CONV_WIDTH), row),
            pl.BlockSpec((POST_TILE, D_MODEL), row),
            pl.BlockSpec((None, CONV_K, CONV_WIDTH), lambda i: (layer, 0, 0)),
            vec(ATTN_WIDTH),
            vec(CONV_WIDTH),
            resident(D_MODEL, D_MODEL),
            vec(D_MODEL),
            vec(D_MODEL),
            resident(D_MODEL, 2 * FFN_HIDDEN),
            resident(FFN_HIDDEN, D_MODEL),
            vec(D_MODEL),
        ],
        out_specs=pl.BlockSpec((POST_TILE, D_MODEL), row),
        out_shape=jax.ShapeDtypeStruct((n, D_MODEL), F32),
        compiler_params=pltpu.CompilerParams(
            dimension_semantics=("parallel",), vmem_limit_bytes=58 * 2 ** 20),
        name="post",
    )(attn2, u, u, u, gb, x2, conv_w, g_attn, g_conv, w_out, g_post, g_pre, w_gu, w_down, g_post2)


def _rotary_tables(positions):
    half = ROPE_DIM // 2
    inv_freq = ROPE_THETA ** (-jnp.arange(0, ROPE_DIM, 2, dtype=F32) / ROPE_DIM)
    in_head = jnp.arange(LANES) % HEAD_DIM
    ang = positions.astype(F32).reshape(-1, 1) * inv_freq
    cos = jnp.tile(jnp.cos(ang), (1, LANES // half))
    sin = jnp.tile(jnp.sin(ang), (1, LANES // half))
    c = jnp.where(in_head < ROPE_DIM, cos, 1.0)
    s_up = jnp.where((in_head >= half) & (in_head < ROPE_DIM), sin, 0.0)
    s_dn = jnp.where(in_head < half, -sin, 0.0)
    return c, s_up, s_dn


def kernel(x, positions, pre_mix_norm, w_in, conv_w, attn_out_norm, conv_out_norm, w_out,
           post_mix_norm, pre_ffn_norm, w_gate_up, w_down, post_ffn_norm):
    b, s, d = x.shape
    depth = w_in.shape[0]
    n = b * s
    cos, sin_up, sin_dn = _rotary_tables(positions)
    vec = lambda g: g.reshape(depth, 1, g.shape[-1])
    w_in_b = w_in[0].astype(BF16)
    x2 = x.reshape(n, d)
    for l in range(depth):
        casts = [(w_out, l), (w_gate_up, l), (w_down, l)] + ([(w_in, l + 1)] if l + 1 < depth else [])
        (q, k, v, u, gb), cast = _in_proj(x2, vec(pre_mix_norm), w_in_b, cos, sin_up, sin_dn, l,
                                          casts)
        w_out_b, w_gu_b, w_down_b = cast[:3]
        w_in_b = cast[3] if l + 1 < depth else None
        attn = _attention(q.reshape(b, s, ATTN_WIDTH), k.reshape(b, s, ATTN_WIDTH),
                          v.reshape(b, s, ATTN_WIDTH))
        x2 = _post(attn.reshape(n, ATTN_WIDTH), u, gb, x2, conv_w, vec(attn_out_norm),
                   vec(conv_out_norm), w_out_b, vec(post_mix_norm), vec(pre_ffn_norm), w_gu_b,
                   w_down_b, vec(post_ffn_norm), l, s)
    return x2.reshape(b, s, d)
```
